```python
import math
import jax
import jax.numpy as jnp
from jax import lax
import numpy as np

D_MODEL = 1024
BATCH = 16
SEQ = 256
DEPTH = 2
DEC_BATCH = 8
DEC_SEQ = 2048
PAST_LEN = 256

F32 = jnp.float32
GRID_W = 64
N_MOD = 6
EPS = 1e-6
GN_EPS = 64e-5
A_HEADS = 8
A_HD = 64
A_W = A_HEADS * A_HD
A_LORA_W = 64
A_LORA_A = 64
A_LORA_G = 128
A_DECAY_SCALE = 0.606531
B_HEADS = 4
B_HD = 128
B_W = B_HEADS * B_HD
B_CONV = 5
B_CHUNK = 64
C_HEADS = 8
C_KV = 2
C_HD = 64
C_W = C_HEADS * C_HD
C_KVW = C_KV * C_HD
C_QBLOCK = 128
ROPE_BASE = 10000.0
N_BRANCH = 3
IN_SPLITS = (A_W, A_W, A_W, 2 * A_LORA_W, 2 * A_LORA_A, A_LORA_G,
             B_W, B_W, B_W, 2 * B_HEADS, 2 * B_HEADS, B_W,
             C_W, C_KVW, C_KVW, N_BRANCH * D_MODEL)
N_IN = 3 * A_W + 2 * A_LORA_W + 2 * A_LORA_A + A_LORA_G + 4 * B_W + 4 * B_HEADS + C_W + 2 * C_KVW + N_BRANCH * D_MODEL
P_HEADS = 8
P_NKEYS = 128
P_EXPERTS = P_NKEYS * P_NKEYS
P_TOPK = 16
P_DKEY = 256
P_DHALF = P_DKEY // 2
P_BLOCK = 128

kernel_name = 'hybrid_rwkv7_gdn_gqa_peer_prefix_diffusion_step'


def rms_norm(x, g):
    x32 = x.astype(F32)
    y = x32 * lax.rsqrt(jnp.mean(x32 * x32, axis=-1, keepdims=True) + EPS)
    return (y * g.astype(F32)).astype(x.dtype)


def l2_normalize(x):
    x32 = x.astype(F32)
    return x32 * lax.rsqrt(jnp.sum(x32 * x32, axis=-1, keepdims=True) + EPS)


def split_cols(p, sizes):
    cuts = np.cumsum(np.asarray(sizes))[:-1].tolist()
    return jnp.split(p, cuts, axis=-1)


def axial_rope(x):
    B, T, H, D = x.shape
    n_rows = T // GRID_W
    n_freq = D // 4
    rows = jnp.repeat(jnp.arange(n_rows, dtype=F32), GRID_W)
    cols = jnp.tile(jnp.arange(GRID_W, dtype=F32), n_rows)
    inv = ROPE_BASE ** (-jnp.arange(n_freq, dtype=F32) / n_freq)
    ang = jnp.stack([rows[:, None] * inv, cols[:, None] * inv], axis=1)
    cos = jnp.cos(ang)[None, :, None]
    sin = jnp.sin(ang)[None, :, None]
    xr = x.astype(F32).reshape(B, T, H, 2, 2, n_freq)
    x1 = xr[..., 0, :]
    x2 = xr[..., 1, :]
    out = jnp.stack([x1 * cos - x2 * sin, x2 * cos + x1 * sin], axis=-2)
    return out.reshape(B, T, H, D).astype(x.dtype)


def blocked_attention(q, k, v):
    B, T, H, D = q.shape
    nb = T // C_QBLOCK
    grp = H // C_KV
    qb = jnp.moveaxis(q.reshape(B, nb, C_QBLOCK, C_KV, grp, D), 1, 0)
    scale = D ** -0.5

    def one_block(qi):
        s = jnp.einsum('bqkgd,bskd->bkgqs', qi, k).astype(F32) * scale
        p = jax.nn.softmax(s, axis=-1).astype(v.dtype)
        return jnp.einsum('bkgqs,bskd->bqkgd', p, v)

    o = lax.map(one_block, qb)
    return jnp.moveaxis(o, 0, 1).reshape(B, T, H, D)


def rwkv7_scan(r, w, k, v, aa, bb, s0, reverse):
    def step(S, inp):
        r_t, w_t, k_t, v_t, a_t, b_t = inp
        sa = jnp.einsum('bhvk,bhk->bhv', S, a_t)
        S = S * w_t[:, :, None, :] + sa[..., None] * b_t[:, :, None, :] + v_t[..., None] * k_t[:, :, None, :]
        return S, jnp.einsum('bhvk,bhk->bhv', S, r_t)

    xs = tuple(jnp.moveaxis(t, 1, 0) for t in (r, w, k, v, aa, bb))
    s_fin, y = lax.scan(step, s0, xs, reverse=reverse)
    return jnp.moveaxis(y, 0, 1), s_fin


def rwkv7_branch(pr, pk, pv, pw, pa, pg, lp, s0):
    B, T, _ = pr.shape

    def heads(t):
        return t.astype(F32).reshape(B, T, A_HEADS, A_HD)

    r, k, v = heads(pr), heads(pk), heads(pv)
    k_k = lp['rk_kk'].astype(F32).reshape(A_HEADS, A_HD)
    k_a = lp['rk_ka'].astype(F32).reshape(A_HEADS, A_HD)
    r_k = lp['rk_rk'].astype(F32)
    kk = l2_normalize(k * k_k)
    pw = pw.astype(F32).reshape(B, T, 2, A_LORA_W)
    pa = pa.astype(F32).reshape(B, T, 2, A_LORA_A)
    y = jnp.zeros_like(v)
    finals = []
    for d in range(2):
        decay = jnp.exp(-A_DECAY_SCALE * jax.nn.sigmoid(lp['rk_w0'][d] + jnp.tanh(pw[:, :, d]) @ lp['rk_w_up'][d]))
        a = heads(jax.nn.sigmoid(lp['rk_a0'][d] + pa[:, :, d] @ lp['rk_a_up'][d]))
        k_d = k * (1.0 + (a - 1.0) * k_a)
        y_d, s_d = rwkv7_scan(r, heads(decay), k_d, v, -kk, kk * a, s0[:, d], reverse=(d == 1))
        y = y + y_d + jnp.sum(r * k_d * r_k, axis=-1, keepdims=True) * v
        finals.append(s_d)
    mu = jnp.mean(y, axis=-1, keepdims=True)
    var = jnp.mean(jnp.square(y - mu), axis=-1, keepdims=True)
    yn = ((y - mu) * lax.rsqrt(var + GN_EPS)).reshape(B, T, A_W) * lp['rk_ln_g'] + lp['rk_ln_b']
    g = jax.nn.sigmoid(pg.astype(F32)) @ lp['rk_g_up']
    return (yn * g).astype(pr.dtype), jnp.stack(finals, axis=1)


def short_conv(x, w):
    C = x.shape[-1]
    y = lax.conv_general_dilated(x, w[:, None, :].astype(x.dtype), window_strides=(1,),
                                 padding=[(B_CONV // 2, B_CONV // 2)],
                                 dimension_numbers=('NWC', 'WIO', 'NWC'), feature_group_count=C)
    return jax.nn.silu(y)


def gdn_chunk(q, k, v, g, beta, s0):
    B, T, H, _ = q.shape
    DV = v.shape[-1]
    n = T // B_CHUNK

    def chunks(t):
        t = t.reshape((B, n, B_CHUNK, H) + t.shape[3:])
        return jnp.moveaxis(t, 3, 1)

    q, k, v, g, beta = (chunks(t) for t in (q, k, v, g, beta))
    gc = jnp.cumsum(g, axis=-1)
    idx = jnp.arange(B_CHUNK)
    lower = idx[:, None] >= idx[None, :]
    strict = idx[:, None] > idx[None, :]
    decay = jnp.exp(jnp.where(lower, gc[..., :, None] - gc[..., None, :], -jnp.inf))
    kb = k * beta[..., None]
    lmat = jnp.where(strict, jnp.einsum('bhncd,bhnsd->bhncs', kb, k) * decay, 0.0)
    tmat = lmat + jnp.eye(B_CHUNK, dtype=lmat.dtype)
    u = lax.linalg.triangular_solve(tmat, v * beta[..., None], left_side=True, lower=True, unit_diagonal=True)
    wk = lax.linalg.triangular_solve(tmat, kb * jnp.exp(gc)[..., None], left_side=True, lower=True, unit_diagonal=True)
    aqk = jnp.where(lower, jnp.einsum('bhncd,bhnsd->bhncs', q, k) * decay, 0.0)

    def step(S, inp):
        q_i, k_i, u_i, w_i, gc_i, a_i = inp
        v_new = u_i - jnp.einsum('bhck,bhkv->bhcv', w_i, S)
        o = jnp.einsum('bhck,bhkv->bhcv', q_i * jnp.exp(gc_i)[..., None], S) + jnp.einsum('bhcs,bhsv->bhcv', a_i, v_new)
        g_last = gc_i[..., -1]
        S = S * jnp.exp(g_last)[..., None, None] + jnp.einsum(
            'bhck,bhcv->bhkv', k_i * jnp.exp(g_last[..., None] - gc_i)[..., None], v_new)
        return S, o

    xs = tuple(jnp.moveaxis(t, 2, 0) for t in (q, k, u, wk, gc, aqk))
    s_fin, o = lax.scan(step, s0, xs)
    o = jnp.moveaxis(o, 0, 2).reshape(B, H, T, DV)
    return jnp.swapaxes(o, 1, 2), s_fin


def gdn_branch(pq, pk, pv, p_alpha, p_beta, pz, lp, s0):
    B, T, _ = pq.shape
    qkv = short_conv(jnp.concatenate([pq, pk, pv], axis=-1), lp['gd_conv'])
    q, k, v = jnp.split(qkv, 3, axis=-1)
    q = l2_normalize(q.reshape(B, T, B_HEADS, B_HD)) * (B_HD ** -0.5)
    k = l2_normalize(k.reshape(B, T, B_HEADS, B_HD))
    v = v.reshape(B, T, B_HEADS, B_HD).astype(F32)
    alpha = p_alpha.astype(F32).reshape(B, T, 2, B_HEADS)
    beta_in = p_beta.astype(F32).reshape(B, T, 2, B_HEADS)

    def rev(t):
        return jnp.flip(t, axis=1)

    o = jnp.zeros_like(v)
    finals = []
    for d in range(2):
        g = -jnp.exp(lp['gd_a_log'][d].astype(F32)) * jax.nn.softplus(alpha[:, :, d] + lp['gd_dt_bias'][d])
        beta = jax.nn.sigmoid(beta_in[:, :, d])
        if d == 0:
            o_d, s_d = gdn_chunk(q, k, v, g, beta, s0[:, d])
        else:
            o_d, s_d = gdn_chunk(rev(q), rev(k), rev(v), rev(g), rev(beta), s0[:, d])
            o_d = rev(o_d)
        o = o + o_d
        finals.append(s_d)
    z = jax.nn.silu(pz.reshape(B, T, B_HEADS, B_HD))
    y = rms_norm(o, lp['gd_norm_g']) * z
    return y.reshape(B, T, B_W).astype(pq.dtype), jnp.stack(finals, axis=1)


def attention_branch(pq, pk, pv, lp, ctx_kv):
    B, T, _ = pq.shape
    q = rms_norm(pq.reshape(B, T, C_HEADS, C_HD), lp['at_q_g'])
    k = rms_norm(pk.reshape(B, T, C_KV, C_HD), lp['at_k_g'])
    v = pv.reshape(B, T, C_KV, C_HD)
    if ctx_kv is None:
        o = blocked_attention(q, k, v)
    else:
        keys = jnp.concatenate([ctx_kv[0].astype(k.dtype), axial_rope(k)], axis=1)
        vals = jnp.concatenate([ctx_kv[1].astype(v.dtype), v], axis=1)
        o = blocked_attention(axial_rope(q), keys, vals)
    return o.reshape(B, T, C_W), k, v


def peer(h, wq, sub_keys, u_tab, v_tab):
    B, T, D = h.shape
    xt = h.reshape(-1, P_BLOCK, D)

    def one_block(xb):
        n = xb.shape[0]
        q = (xb @ wq).reshape(n, P_HEADS, 2, P_DHALF)
        s = jnp.einsum('nhpd,hpkd->nhpk', q, sub_keys).astype(F32)
        s_top, i_top = lax.top_k(s, P_TOPK)
        cand_s = (s_top[:, :, 0, :, None] + s_top[:, :, 1, None, :]).reshape(n, P_HEADS, P_TOPK * P_TOPK)
        cand_i = (i_top[:, :, 0, :, None] * P_NKEYS + i_top[:, :, 1, None, :]).reshape(n, P_HEADS, P_TOPK * P_TOPK)
        best_s, pos = lax.top_k(cand_s, P_TOPK)
        idx = jnp.take_along_axis(cand_i, pos, axis=-1).reshape(n, P_HEADS * P_TOPK)
        gate = jax.nn.softmax(best_s, axis=-1).reshape(n, P_HEADS * P_TOPK).astype(xb.dtype)
        hid = jax.nn.gelu(jnp.einsum('ned,nd->ne', u_tab[idx], xb))
        return jnp.einsum('ne,ned->nd', gate * hid, v_tab[idx])

    return lax.map(one_block, xt).reshape(B, T, D)


def trunk_layer(x, cvec, lp, ctx):
    B, T, _ = x.shape
    mod = jax.nn.silu(cvec) @ lp['w_mod'] + lp['b_mod']
    sh1, sc1, g1, sh2, sc2, g2 = [m[:, None, :] for m in jnp.split(mod, N_MOD, axis=-1)]
    h = rms_norm(x, lp['norm1_g']) * (1 + sc1) + sh1
    (ar, ak, av, aw, aa, ag, bq, bk, bv, b_alpha, b_beta, bz, cq, ck, cv, gl) = split_cols(h @ lp['w_in'], IN_SPLITS)
    if ctx is None:
        s_rwkv0 = jnp.zeros((B, 2, A_HEADS, A_HD, A_HD), F32)
        s_gdn0 = jnp.zeros((B, 2, B_HEADS, B_HD, B_HD), F32)
        ctx_kv = None
    else:
        s_rwkv0 = ctx['rwkv'].astype(F32)
        s_gdn0 = ctx['gdn'].astype(F32)
        ctx_kv = (ctx['k'], ctx['v'])
    ya, s_rwkv = rwkv7_branch(ar, ak, av, aw, aa, ag, lp, s_rwkv0)
    yb, s_gdn = gdn_branch(bq, bk, bv, b_alpha, b_beta, bz, lp, s_gdn0)
    yc, k_c, v_c = attention_branch(cq, ck, cv, lp, ctx_kv)
    gate = jax.nn.sigmoid(gl.reshape(B, T, N_BRANCH, D_MODEL))
    merged = (gate[:, :, 0] * (ya @ lp['w_br_a']) + gate[:, :, 1] * (yb @ lp['w_br_b'])
              + gate[:, :, 2] * (yc @ lp['w_br_c']))
    x = x + g1 * (merged @ lp['w_out'])
    h2 = rms_norm(x, lp['norm2_g']) * (1 + sc2) + sh2
    x = x + g2 * peer(h2, lp['pk_wq'], lp['pk_keys'], lp['pk_u'], lp['pk_v'])
    return x, k_c, v_c, s_rwkv, s_gdn


def setup_inputs(seed: int = 0) -> dict:
    key = jax.random.key(seed)
    counter = [0]

    def nk():
        counter[0] += 1
        return jax.random.fold_in(key, counter[0])

    def nrm(shape, scale):
        return jax.random.normal(nk(), shape, F32) * scale

    def gain(shape):
        return 1.0 + 0.01 * jax.random.normal(nk(), shape, F32)

    L = DEPTH
    dt = jnp.exp(jax.random.uniform(nk(), (L, 2, B_HEADS), F32, math.log(1e-3), math.log(1e-1)))
    dt_bias = dt + jnp.log(-jnp.expm1(-dt))
    a_log = jnp.log(jax.random.uniform(nk(), (L, 2, B_HEADS), F32, 1.0, 16.0))
    return {
        'x_prompt': nrm((BATCH, SEQ, D_MODEL), 1.0),
        'x_sample': nrm((DEC_BATCH, DEC_SEQ, D_MODEL), 1.0),
        'cache_attn_k': nrm((DEC_BATCH, DEPTH, PAST_LEN, C_KV, C_HD), 1.0),
        'cache_attn_v': nrm((DEC_BATCH, DEPTH, PAST_LEN, C_KV, C_HD), 1.0),
        'state_rwkv': nrm((DEC_BATCH, DEPTH, 2, A_HEADS, A_HD, A_HD), 0.1),
        'state_gdn': nrm((DEC_BATCH, DEPTH, 2, B_HEADS, B_HD, B_HD), 0.1),
        'c': nrm((DEC_BATCH, D_MODEL), 1.0),
        'c_ctx': nrm((D_MODEL,), 1.0),
        'w_mod': nrm((L, D_MODEL, N_MOD * D_MODEL), D_MODEL ** -0.5),
        'b_mod': nrm((L, N_MOD * D_MODEL), 0.01),
        'norm1_g': gain((L, D_MODEL)),
        'w_in': nrm((L, D_MODEL, N_IN), D_MODEL ** -0.5),
        'rk_w_up': nrm((L, 2, A_LORA_W, A_W), A_LORA_W ** -0.5),
        'rk_w0': nrm((L, 2, A_W), 0.5),
        'rk_a_up': nrm((L, 2, A_LORA_A, A_W), A_LORA_A ** -0.5),
        'rk_a0': nrm((L, 2, A_W), 0.5),
        'rk_g_up': nrm((L, A_LORA_G, A_W), A_LORA_G ** -0.5),
        'rk_kk': 1.0 + nrm((L, A_W), 0.1),
        'rk_ka': 1.0 + nrm((L, A_W), 0.1),
        'rk_rk': nrm((L, A_HEADS, A_HD), 0.1),
        'rk_ln_g': gain((L, A_W)),
        'rk_ln_b': nrm((L, A_W), 0.01),
        'gd_conv': nrm((L, B_CONV, 3 * B_W), B_CONV ** -0.5),
        'gd_a_log': a_log,
        'gd_dt_bias': dt_bias,
        'gd_norm_g': gain((L, B_HD)),
        'at_q_g': gain((L, C_HD)),
        'at_k_g': gain((L, C_HD)),
        'w_br_a': nrm((L, A_W, D_MODEL), A_W ** -0.5),
        'w_br_b': nrm((L, B_W, D_MODEL), B_W ** -0.5),
        'w_br_c': nrm((L, C_W, D_MODEL), C_W ** -0.5),
        'w_out': nrm((L, D_MODEL, D_MODEL), D_MODEL ** -0.5),
        'norm2_g': gain((L, D_MODEL)),
        'pk_wq': nrm((L, D_MODEL, P_HEADS * P_DKEY), D_MODEL ** -0.5),
        'pk_keys': nrm((L, P_HEADS, 2, P_NKEYS, P_DHALF), P_DHALF ** -0.5),
        'pk_u': nrm((L, P_EXPERTS, D_MODEL), D_MODEL ** -0.5),
        'pk_v': nrm((L, P_EXPERTS, D_MODEL), 0.5),
        'final_g': gain((D_MODEL,)),
    }


def reference(x_prompt, x_sample, cache_attn_k, cache_attn_v, state_rwkv, state_gdn, c, c_ctx,
              w_mod, b_mod, norm1_g, w_in,
              rk_w_up, rk_w0, rk_a_up, rk_a0, rk_g_up, rk_kk, rk_ka, rk_rk, rk_ln_g, rk_ln_b,
              gd_conv, gd_a_log, gd_dt_bias, gd_norm_g,
              at_q_g, at_k_g,
              w_br_a, w_br_b, w_br_c, w_out,
              norm2_g, pk_wq, pk_keys, pk_u, pk_v, final_g):
    xp = x_prompt
    xs = x_sample
    k_list, v_list, sr_list, sg_list = [], [], [], []
    for l in range(DEPTH):
        lp = {
            'w_mod': w_mod[l], 'b_mod': b_mod[l], 'norm1_g': norm1_g[l], 'w_in': w_in[l],
            'rk_w_up': rk_w_up[l], 'rk_w0': rk_w0[l], 'rk_a_up': rk_a_up[l], 'rk_a0': rk_a0[l],
            'rk_g_up': rk_g_up[l], 'rk_kk': rk_kk[l], 'rk_ka': rk_ka[l], 'rk_rk': rk_rk[l],
            'rk_ln_g': rk_ln_g[l], 'rk_ln_b': rk_ln_b[l],
            'gd_conv': gd_conv[l], 'gd_a_log': gd_a_log[l], 'gd_dt_bias': gd_dt_bias[l], 'gd_norm_g': gd_norm_g[l],
            'at_q_g': at_q_g[l], 'at_k_g': at_k_g[l],
            'w_br_a': w_br_a[l], 'w_br_b': w_br_b[l], 'w_br_c': w_br_c[l], 'w_out': w_out[l],
            'norm2_g': norm2_g[l], 'pk_wq': pk_wq[l], 'pk_keys': pk_keys[l], 'pk_u': pk_u[l], 'pk_v': pk_v[l],
        }
        xp, k_c, v_c, s_r, s_g = trunk_layer(xp, c_ctx[None, :], lp, None)
        k_list.append(k_c)
        v_list.append(v_c)
        sr_list.append(s_r)
        sg_list.append(s_g)
        ctx = {'k': cache_attn_k[:, l], 'v': cache_attn_v[:, l], 'rwkv': state_rwkv[:, l], 'gdn': state_gdn[:, l]}
        xs = trunk_layer(xs, c, lp, ctx)[0]
    y_prompt = rms_norm(xp, final_g)
    y_sample = rms_norm(xs, final_g)
    new_cache_attn_k = jnp.stack(k_list, axis=1)
    new_cache_attn_v = jnp.stack(v_list, axis=1)
    new_state_rwkv = jnp.stack(sr_list, axis=1).astype(x_prompt.dtype)
    new_state_gdn = jnp.stack(sg_list, axis=1).astype(x_prompt.dtype)
    return (y_prompt, y_sample, new_cache_attn_k, new_cache_attn_v, new_state_rwkv, new_state_gdn)
```

```python
import functools
import math

import jax
import jax.numpy as jnp
import numpy as np
from jax import lax
from jax.experimental import pallas as pl
from jax.experimental.pallas import tpu as pltpu

F32 = jnp.float32
BF16 = jnp.bfloat16

D_MODEL = 1024
DEPTH = 2
GRID_W = 64
N_MOD = 6
EPS = 1e-6
GN_EPS = 64e-5
A_HEADS = 8
A_HD = 64
A_W = A_HEADS * A_HD
A_LORA_W = 64
A_LORA_A = 64
A_LORA_G = 128
A_DECAY_SCALE = 0.606531
B_HEADS = 4
B_HD = 128
B_W = B_HEADS * B_HD
B_CONV = 5
B_CHUNK = 64
C_HEADS = 8
C_KV = 2
C_HD = 64
C_W = C_HEADS * C_HD
C_KVW = C_KV * C_HD
C_QBLOCK = 128
ROPE_BASE = 10000.0
N_BRANCH = 3
IN_SPLITS = (A_W, A_W, A_W, 2 * A_LORA_W, 2 * A_LORA_A, A_LORA_G,
             B_W, B_W, B_W, 2 * B_HEADS, 2 * B_HEADS, B_W,
             C_W, C_KVW, C_KVW, N_BRANCH * D_MODEL)
N_IN = sum(IN_SPLITS)
P_HEADS = 8
P_NKEYS = 128
P_TOPK = 16
P_DKEY = 256
P_DHALF = P_DKEY // 2
P_BLOCK = 128

V7X_VMEM_LIMIT_BYTES = 48 * 1024 * 1024
LANE = 128


def _mm_kernel(x_ref, w_ref, o_ref):
    o_ref[...] = jnp.dot(x_ref[...].astype(BF16), w_ref[...], preferred_element_type=F32)


def _pick_tile(n, candidates):
    for c in candidates:
        if n % c == 0:
            return c
    return n


def pmm(x, w):
    M, K = x.shape
    N = w.shape[1]
    n_pad = (-N) % LANE
    wb = w.astype(BF16)
    if n_pad:
        wb = jnp.pad(wb, ((0, 0), (0, n_pad)))
    Np = N + n_pad
    tm = _pick_tile(M, (1024, 512, 256, 128, 64, 32, 16, 8))
    tn = _pick_tile(Np, (512, 256, 128))
    out = pl.pallas_call(
        _mm_kernel,
        grid=(M // tm, Np // tn),
        in_specs=[pl.BlockSpec((tm, K), lambda i, j: (i, 0)),
                  pl.BlockSpec((K, tn), lambda i, j: (0, j))],
        out_specs=pl.BlockSpec((tm, tn), lambda i, j: (i, j)),
        out_shape=jax.ShapeDtypeStruct((M, Np), F32),
        compiler_params=pltpu.CompilerParams(
            dimension_semantics=("arbitrary", "arbitrary"),
            vmem_limit_bytes=V7X_VMEM_LIMIT_BYTES),
        name="pmm",
    )(x, wb)
    return out[:, :N] if n_pad else out


def pmm3(x, w):
    B, T, K = x.shape
    return pmm(x.reshape(B * T, K), w).reshape(B, T, w.shape[1])


def rms_norm(x, g):
    x32 = x.astype(F32)
    y = x32 * lax.rsqrt(jnp.mean(x32 * x32, axis=-1, keepdims=True) + EPS)
    return (y * g.astype(F32)).astype(x.dtype)


def l2_normalize(x):
    x32 = x.astype(F32)
    return x32 * lax.rsqrt(jnp.sum(x32 * x32, axis=-1, keepdims=True) + EPS)


def split_cols(p, sizes):
    cuts = np.cumsum(np.asarray(sizes))[:-1].tolist()
    return jnp.split(p, cuts, axis=-1)


def axial_rope(x):
    B, T, H, D = x.shape
    n_rows = T // GRID_W
    n_freq = D // 4
    rows = jnp.repeat(jnp.arange(n_rows, dtype=F32), GRID_W)
    cols = jnp.tile(jnp.arange(GRID_W, dtype=F32), n_rows)
    inv = ROPE_BASE ** (-jnp.arange(n_freq, dtype=F32) / n_freq)
    ang = jnp.stack([rows[:, None] * inv, cols[:, None] * inv], axis=1)
    cos = jnp.cos(ang)[None, :, None]
    sin = jnp.sin(ang)[None, :, None]
    xr = x.astype(F32).reshape(B, T, H, 2, 2, n_freq)
    x1 = xr[..., 0, :]
    x2 = xr[..., 1, :]
    out = jnp.stack([x1 * cos - x2 * sin, x2 * cos + x1 * sin], axis=-2)
    return out.reshape(B, T, H, D).astype(x.dtype)


def blocked_attention(q, k, v):
    B, T, H, D = q.shape
    nb = T // C_QBLOCK
    grp = H // C_KV
    qb = jnp.moveaxis(q.reshape(B, nb, C_QBLOCK, C_KV, grp, D), 1, 0)
    scale = D ** -0.5

    def one_block(qi):
        s = jnp.einsum('bqkgd,bskd->bkgqs', qi, k).astype(F32) * scale
        p = jax.nn.softmax(s, axis=-1).astype(v.dtype)
        return jnp.einsum('bkgqs,bskd->bqkgd', p, v)

    o = lax.map(one_block, qb)
    return jnp.moveaxis(o, 0, 1).reshape(B, T, H, D)


def rwkv7_scan(r, w, k, v, aa, bb, s0, reverse):
    def step(S, inp):
        r_t, w_t, k_t, v_t, a_t, b_t = inp
        sa = jnp.einsum('bhvk,bhk->bhv', S, a_t)
        S = S * w_t[:, :, None, :] + sa[..., None] * b_t[:, :, None, :] + v_t[..., None] * k_t[:, :, None, :]
        return S, jnp.einsum('bhvk,bhk->bhv', S, r_t)

    xs = tuple(jnp.moveaxis(t, 1, 0) for t in (r, w, k, v, aa, bb))
    s_fin, y = lax.scan(step, s0, xs, reverse=reverse)
    return jnp.moveaxis(y, 0, 1), s_fin


def rwkv7_branch(pr, pk, pv, pw, pa, pg, lp, s0):
    B, T, _ = pr.shape

    def heads(t):
        return t.astype(F32).reshape(B, T, A_HEADS, A_HD)

    r, k, v = heads(pr), heads(pk), heads(pv)
    k_k = lp['rk_kk'].astype(F32).reshape(A_HEADS, A_HD)
    k_a = lp['rk_ka'].astype(F32).reshape(A_HEADS, A_HD)
    r_k = lp['rk_rk'].astype(F32)
    kk = l2_normalize(k * k_k)
    pw = pw.astype(F32).reshape(B, T, 2, A_LORA_W)
    pa = pa.astype(F32).reshape(B, T, 2, A_LORA_A)
    y = jnp.zeros_like(v)
    finals = []
    for d in range(2):
        decay = jnp.exp(-A_DECAY_SCALE * jax.nn.sigmoid(lp['rk_w0'][d] + pmm3(jnp.tanh(pw[:, :, d]), lp['rk_w_up'][d])))
        a = heads(jax.nn.sigmoid(lp['rk_a0'][d] + pmm3(pa[:, :, d], lp['rk_a_up'][d])))
        k_d = k * (1.0 + (a - 1.0) * k_a)
        y_d, s_d = rwkv7_scan(r, heads(decay), k_d, v, -kk, kk * a, s0[:, d], reverse=(d == 1))
        y = y + y_d + jnp.sum(r * k_d * r_k, axis=-1, keepdims=True) * v
        finals.append(s_d)
    mu = jnp.mean(y, axis=-1, keepdims=True)
    var = jnp.mean(jnp.square(y - mu), axis=-1, keepdims=True)
    yn = ((y - mu) * lax.rsqrt(var + GN_EPS)).reshape(B, T, A_W) * lp['rk_ln_g'] + lp['rk_ln_b']
    g = pmm3(jax.nn.sigmoid(pg.astype(F32)), lp['rk_g_up'])
    return (yn * g).astype(pr.dtype), jnp.stack(finals, axis=1)


def short_conv(x, w):
    C = x.shape[-1]
    y = lax.conv_general_dilated(x, w[:, None, :].astype(x.dtype), window_strides=(1,),
                                 padding=[(B_CONV // 2, B_CONV // 2)],
                                 dimension_numbers=('NWC', 'WIO', 'NWC'), feature_group_count=C)
    return jax.nn.silu(y)


def gdn_chunk(q, k, v, g, beta, s0):
    B, T, H, _ = q.shape
    DV = v.shape[-1]
    n = T // B_CHUNK

    def chunks(t):
        t = t.reshape((B, n, B_CHUNK, H) + t.shape[3:])
        return jnp.moveaxis(t, 3, 1)

    q, k, v, g, beta = (chunks(t) for t in (q, k, v, g, beta))
    gc = jnp.cumsum(g, axis=-1)
    idx = jnp.arange(B_CHUNK)
    lower = idx[:, None] >= idx[None, :]
    strict = idx[:, None] > idx[None, :]
    decay = jnp.exp(jnp.where(lower, gc[..., :, None] - gc[..., None, :], -jnp.inf))
    kb = k * beta[..., None]
    lmat = jnp.where(strict, jnp.einsum('bhncd,bhnsd->bhncs', kb, k) * decay, 0.0)
    tmat = lmat + jnp.eye(B_CHUNK, dtype=lmat.dtype)
    u = lax.linalg.triangular_solve(tmat, v * beta[..., None], left_side=True, lower=True, unit_diagonal=True)
    wk = lax.linalg.triangular_solve(tmat, kb * jnp.exp(gc)[..., None], left_side=True, lower=True, unit_diagonal=True)
    aqk = jnp.where(lower, jnp.einsum('bhncd,bhnsd->bhncs', q, k) * decay, 0.0)

    def step(S, inp):
        q_i, k_i, u_i, w_i, gc_i, a_i = inp
        v_new = u_i - jnp.einsum('bhck,bhkv->bhcv', w_i, S)
        o = jnp.einsum('bhck,bhkv->bhcv', q_i * jnp.exp(gc_i)[..., None], S) + jnp.einsum('bhcs,bhsv->bhcv', a_i, v_new)
        g_last = gc_i[..., -1]
        S = S * jnp.exp(g_last)[..., None, None] + jnp.einsum(
            'bhck,bhcv->bhkv', k_i * jnp.exp(g_last[..., None] - gc_i)[..., None], v_new)
        return S, o

    xs = tuple(jnp.moveaxis(t, 2, 0) for t in (q, k, u, wk, gc, aqk))
    s_fin, o = lax.scan(step, s0, xs)
    o = jnp.moveaxis(o, 0, 2).reshape(B, H, T, DV)
    return jnp.swapaxes(o, 1, 2), s_fin


def gdn_branch(pq, pk, pv, p_alpha, p_beta, pz, lp, s0):
    B, T, _ = pq.shape
    qkv = short_conv(jnp.concatenate([pq, pk, pv], axis=-1), lp['gd_conv'])
    q, k, v = jnp.split(qkv, 3, axis=-1)
    q = l2_normalize(q.reshape(B, T, B_HEADS, B_HD)) * (B_HD ** -0.5)
    k = l2_normalize(k.reshape(B, T, B_HEADS, B_HD))
    v = v.reshape(B, T, B_HEADS, B_HD).astype(F32)
    alpha = p_alpha.astype(F32).reshape(B, T, 2, B_HEADS)
    beta_in = p_beta.astype(F32).reshape(B, T, 2, B_HEADS)

    def rev(t):
        return jnp.flip(t, axis=1)

    o = jnp.zeros_like(v)
    finals = []
    for d in range(2):
        g = -jnp.exp(lp['gd_a_log'][d].astype(F32)) * jax.nn.softplus(alpha[:, :, d] + lp['gd_dt_bias'][d])
        beta = jax.nn.sigmoid(beta_in[:, :, d])
        if d == 0:
            o_d, s_d = gdn_chunk(q, k, v, g, beta, s0[:, d])
        else:
            o_d, s_d = gdn_chunk(rev(q), rev(k), rev(v), rev(g), rev(beta), s0[:, d])
            o_d = rev(o_d)
        o = o + o_d
        finals.append(s_d)
    z = jax.nn.silu(pz.reshape(B, T, B_HEADS, B_HD))
    y = rms_norm(o, lp['gd_norm_g']) * z
    return y.reshape(B, T, B_W).astype(pq.dtype), jnp.stack(finals, axis=1)


def attention_branch(pq, pk, pv, lp, ctx_kv):
    B, T, _ = pq.shape
    q = rms_norm(pq.reshape(B, T, C_HEADS, C_HD), lp['at_q_g'])
    k = rms_norm(pk.reshape(B, T, C_KV, C_HD), lp['at_k_g'])
    v = pv.reshape(B, T, C_KV, C_HD)
    if ctx_kv is None:
        o = blocked_attention(q, k, v)
    else:
        keys = jnp.concatenate([ctx_kv[0].astype(k.dtype), axial_rope(k)], axis=1)
        vals = jnp.concatenate([ctx_kv[1].astype(v.dtype), v], axis=1)
        o = blocked_attention(axial_rope(q), keys, vals)
    return o.reshape(B, T, C_W), k, v


def peer(h, wq, sub_keys, u_tab, v_tab):
    B, T, D = h.shape
    xt = h.reshape(-1, P_BLOCK, D)
    qall = pmm(h.reshape(-1, D), wq).reshape(-1, P_BLOCK, P_HEADS * P_DKEY)

    def one_block(args):
        xb, qb = args
        n = xb.shape[0]
        q = qb.reshape(n, P_HEADS, 2, P_DHALF)
        s = jnp.einsum('nhpd,hpkd->nhpk', q, sub_keys).astype(F32)
        s_top, i_top = lax.top_k(s, P_TOPK)
        cand_s = (s_top[:, :, 0, :, None] + s_top[:, :, 1, None, :]).reshape(n, P_HEADS, P_TOPK * P_TOPK)
        cand_i = (i_top[:, :, 0, :, None] * P_NKEYS + i_top[:, :, 1, None, :]).reshape(n, P_HEADS, P_TOPK * P_TOPK)
        best_s, pos = lax.top_k(cand_s, P_TOPK)
        idx = jnp.take_along_axis(cand_i, pos, axis=-1).reshape(n, P_HEADS * P_TOPK)
        gate = jax.nn.softmax(best_s, axis=-1).reshape(n, P_HEADS * P_TOPK).astype(xb.dtype)
        hid = jax.nn.gelu(jnp.einsum('ned,nd->ne', u_tab[idx], xb))
        return jnp.einsum('ne,ned->nd', gate * hid, v_tab[idx])

    return lax.map(one_block, (xt, qall)).reshape(B, T, D)


def trunk_layer(x, cvec, lp, ctx):
    B, T, _ = x.shape
    mod = jax.nn.silu(cvec) @ lp['w_mod'] + lp['b_mod']
    sh1, sc1, g1, sh2, sc2, g2 = [m[:, None, :] for m in jnp.split(mod, N_MOD, axis=-1)]
    h = rms_norm(x, lp['norm1_g']) * (1 + sc1) + sh1
    (ar, ak, av, aw, aa, ag, bq, bk, bv, b_alpha, b_beta, bz, cq, ck, cv, gl) = split_cols(
        pmm3(h, lp['w_in']), IN_SPLITS)
    if ctx is None:
        s_rwkv0 = jnp.zeros((B, 2, A_HEADS, A_HD, A_HD), F32)
        s_gdn0 = jnp.zeros((B, 2, B_HEADS, B_HD, B_HD), F32)
        ctx_kv = None
    else:
        s_rwkv0 = ctx['rwkv'].astype(F32)
        s_gdn0 = ctx['gdn'].astype(F32)
        ctx_kv = (ctx['k'], ctx['v'])
    ya, s_rwkv = rwkv7_branch(ar, ak, av, aw, aa, ag, lp, s_rwkv0)
    yb, s_gdn = gdn_branch(bq, bk, bv, b_alpha, b_beta, bz, lp, s_gdn0)
    yc, k_c, v_c = attention_branch(cq, ck, cv, lp, ctx_kv)
    gate = jax.nn.sigmoid(gl.reshape(B, T, N_BRANCH, D_MODEL))
    merged = (gate[:, :, 0] * pmm3(ya, lp['w_br_a']) + gate[:, :, 1] * pmm3(yb, lp['w_br_b'])
              + gate[:, :, 2] * pmm3(yc, lp['w_br_c']))
    x = x + g1 * pmm3(merged, lp['w_out'])
    h2 = rms_norm(x, lp['norm2_g']) * (1 + sc2) + sh2
    x = x + g2 * peer(h2, lp['pk_wq'], lp['pk_keys'], lp['pk_u'], lp['pk_v'])
    return x, k_c, v_c, s_rwkv, s_gdn


_PARAM_NAMES = ('w_mod', 'b_mod', 'norm1_g', 'w_in',
                'rk_w_up', 'rk_w0', 'rk_a_up', 'rk_a0', 'rk_g_up', 'rk_kk', 'rk_ka', 'rk_rk', 'rk_ln_g', 'rk_ln_b',
                'gd_conv', 'gd_a_log', 'gd_dt_bias', 'gd_norm_g',
                'at_q_g', 'at_k_g',
                'w_br_a', 'w_br_b', 'w_br_c', 'w_out',
                'norm2_g', 'pk_wq', 'pk_keys', 'pk_u', 'pk_v')


def kernel(x_prompt, x_sample, cache_attn_k, cache_attn_v, state_rwkv, state_gdn, c, c_ctx, w_mod, b_mod, norm1_g, w_in, rk_w_up, rk_w0, rk_a_up, rk_a0, rk_g_up, rk_kk, rk_ka, rk_rk, rk_ln_g, rk_ln_b, gd_conv, gd_a_log, gd_dt_bias, gd_norm_g, at_q_g, at_k_g, w_br_a, w_br_b, w_br_c, w_out, norm2_g, pk_wq, pk_keys, pk_u, pk_v, final_g):
    stacked = dict(zip(_PARAM_NAMES, (
        w_mod, b_mod, norm1_g, w_in,
        rk_w_up, rk_w0, rk_a_up, rk_a0, rk_g_up, rk_kk, rk_ka, rk_rk, rk_ln_g, rk_ln_b,
        gd_conv, gd_a_log, gd_dt_bias, gd_norm_g,
        at_q_g, at_k_g,
        w_br_a, w_br_b, w_br_c, w_out,
        norm2_g, pk_wq, pk_keys, pk_u, pk_v)))
    xp = x_prompt
    xs = x_sample
    k_list, v_list, sr_list, sg_list = [], [], [], []
    for l in range(DEPTH):
        lp = {name: arr[l] for name, arr in stacked.items()}
        xp, k_c, v_c, s_r, s_g = trunk_layer(xp, c_ctx[None, :], lp, None)
        k_list.append(k_c)
        v_list.append(v_c)
        sr_list.append(s_r)
        sg_list.append(s_g)
        ctx = {'k': cache_attn_k[:, l], 'v': cache_attn_v[:, l], 'rwkv': state_rwkv[:, l], 'gdn': state_gdn[:, l]}
        xs = trunk_layer(xs, c, lp, ctx)[0]
    y_prompt = rms_norm(xp, final_g)
    y_sample = rms_norm(xs, final_g)
    return (y_prompt, y_sample, jnp.stack(k_list, axis=1), jnp.stack(v_list, axis=1),
            jnp.stack(sr_list, axis=1).astype(x_prompt.dtype), jnp.stack(sg_list, axis=1).astype(x_prompt.dtype))
```

```python
import functools
import math

import jax
import jax.numpy as jnp
import numpy as np
from jax import lax
from jax.experimental import pallas as pl
from jax.experimental.pallas import tpu as pltpu

F32 = jnp.float32
BF16 = jnp.bfloat16

D_MODEL = 1024
DEPTH = 2
GRID_W = 64
N_MOD = 6
EPS = 1e-6
GN_EPS = 64e-5
A_HEADS = 8
A_HD = 64
A_W = A_HEADS * A_HD
A_LORA_W = 64
A_LORA_A = 64
A_LORA_G = 128
A_DECAY_SCALE = 0.606531
B_HEADS = 4
B_HD = 128
B_W = B_HEADS * B_HD
B_CONV = 5
B_CHUNK = 64
C_HEADS = 8
C_KV = 2
C_HD = 64
C_W = C_HEADS * C_HD
C_KVW = C_KV * C_HD
C_QBLOCK = 128
ROPE_BASE = 10000.0
N_BRANCH = 3
IN_SPLITS = (A_W, A_W, A_W, 2 * A_LORA_W, 2 * A_LORA_A, A_LORA_G,
             B_W, B_W, B_W, 2 * B_HEADS, 2 * B_HEADS, B_W,
             C_W, C_KVW, C_KVW, N_BRANCH * D_MODEL)
N_IN = sum(IN_SPLITS)
P_HEADS = 8
P_NKEYS = 128
P_TOPK = 16
P_DKEY = 256
P_DHALF = P_DKEY // 2
P_BLOCK = 128

V7X_VMEM_LIMIT_BYTES = 48 * 1024 * 1024
LANE = 128


def _mm_kernel(x_ref, w_ref, o_ref):
    o_ref[...] = jnp.dot(x_ref[...].astype(BF16), w_ref[...], preferred_element_type=F32)


def _pick_tile(n, candidates):
    for c in candidates:
        if n % c == 0:
            return c
    return n


def pmm(x, w):
    M, K = x.shape
    N = w.shape[1]
    n_pad = (-N) % LANE
    wb = w.astype(BF16)
    if n_pad:
        wb = jnp.pad(wb, ((0, 0), (0, n_pad)))
    Np = N + n_pad
    tm = _pick_tile(M, (1024, 512, 256, 128, 64, 32, 16, 8))
    tn = _pick_tile(Np, (512, 256, 128))
    out = pl.pallas_call(
        _mm_kernel,
        grid=(M // tm, Np // tn),
        in_specs=[pl.BlockSpec((tm, K), lambda i, j: (i, 0)),
                  pl.BlockSpec((K, tn), lambda i, j: (0, j))],
        out_specs=pl.BlockSpec((tm, tn), lambda i, j: (i, j)),
        out_shape=jax.ShapeDtypeStruct((M, Np), F32),
        compiler_params=pltpu.CompilerParams(
            dimension_semantics=("arbitrary", "arbitrary"),
            vmem_limit_bytes=V7X_VMEM_LIMIT_BYTES),
        name="pmm",
    )(x, wb)
    return out[:, :N] if n_pad else out


def pmm3(x, w):
    B, T, K = x.shape
    return pmm(x.reshape(B * T, K), w).reshape(B, T, w.shape[1])


def rms_norm(x, g):
    x32 = x.astype(F32)
    y = x32 * lax.rsqrt(jnp.mean(x32 * x32, axis=-1, keepdims=True) + EPS)
    return (y * g.astype(F32)).astype(x.dtype)


def l2_normalize(x):
    x32 = x.astype(F32)
    return x32 * lax.rsqrt(jnp.sum(x32 * x32, axis=-1, keepdims=True) + EPS)


def split_cols(p, sizes):
    cuts = np.cumsum(np.asarray(sizes))[:-1].tolist()
    return jnp.split(p, cuts, axis=-1)


def axial_rope(x):
    B, T, H, D = x.shape
    n_rows = T // GRID_W
    n_freq = D // 4
    rows = jnp.repeat(jnp.arange(n_rows, dtype=F32), GRID_W)
    cols = jnp.tile(jnp.arange(GRID_W, dtype=F32), n_rows)
    inv = ROPE_BASE ** (-jnp.arange(n_freq, dtype=F32) / n_freq)
    ang = jnp.stack([rows[:, None] * inv, cols[:, None] * inv], axis=1)
    cos = jnp.cos(ang)[None, :, None]
    sin = jnp.sin(ang)[None, :, None]
    xr = x.astype(F32).reshape(B, T, H, 2, 2, n_freq)
    x1 = xr[..., 0, :]
    x2 = xr[..., 1, :]
    out = jnp.stack([x1 * cos - x2 * sin, x2 * cos + x1 * sin], axis=-2)
    return out.reshape(B, T, H, D).astype(x.dtype)


def blocked_attention(q, k, v):
    B, T, H, D = q.shape
    nb = T // C_QBLOCK
    grp = H // C_KV
    qb = jnp.moveaxis(q.reshape(B, nb, C_QBLOCK, C_KV, grp, D), 1, 0)
    scale = D ** -0.5

    def one_block(qi):
        s = jnp.einsum('bqkgd,bskd->bkgqs', qi, k).astype(F32) * scale
        p = jax.nn.softmax(s, axis=-1).astype(v.dtype)
        return jnp.einsum('bkgqs,bskd->bqkgd', p, v)

    o = lax.map(one_block, qb)
    return jnp.moveaxis(o, 0, 1).reshape(B, T, H, D)


def rwkv7_scan(r, w, k, v, aa, bb, s0, reverse):
    def step(S, inp):
        r_t, w_t, k_t, v_t, a_t, b_t = inp
        sa = jnp.einsum('bhvk,bhk->bhv', S, a_t)
        S = S * w_t[:, :, None, :] + sa[..., None] * b_t[:, :, None, :] + v_t[..., None] * k_t[:, :, None, :]
        return S, jnp.einsum('bhvk,bhk->bhv', S, r_t)

    xs = tuple(jnp.moveaxis(t, 1, 0) for t in (r, w, k, v, aa, bb))
    s_fin, y = lax.scan(step, s0, xs, reverse=reverse)
    return jnp.moveaxis(y, 0, 1), s_fin


def rwkv7_branch(pr, pk, pv, pw, pa, pg, lp, s0):
    B, T, _ = pr.shape

    def heads(t):
        return t.astype(F32).reshape(B, T, A_HEADS, A_HD)

    r, k, v = heads(pr), heads(pk), heads(pv)
    k_k = lp['rk_kk'].astype(F32).reshape(A_HEADS, A_HD)
    k_a = lp['rk_ka'].astype(F32).reshape(A_HEADS, A_HD)
    r_k = lp['rk_rk'].astype(F32)
    kk = l2_normalize(k * k_k)
    pw = pw.astype(F32).reshape(B, T, 2, A_LORA_W)
    pa = pa.astype(F32).reshape(B, T, 2, A_LORA_A)
    y = jnp.zeros_like(v)
    finals = []
    for d in range(2):
        decay = jnp.exp(-A_DECAY_SCALE * jax.nn.sigmoid(lp['rk_w0'][d] + pmm3(jnp.tanh(pw[:, :, d]), lp['rk_w_up'][d])))
        a = heads(jax.nn.sigmoid(lp['rk_a0'][d] + pmm3(pa[:, :, d], lp['rk_a_up'][d])))
        k_d = k * (1.0 + (a - 1.0) * k_a)
        y_d, s_d = rwkv7_scan(r, heads(decay), k_d, v, -kk, kk * a, s0[:, d], reverse=(d == 1))
        y = y + y_d + jnp.sum(r * k_d * r_k, axis=-1, keepdims=True) * v
        finals.append(s_d)
    mu = jnp.mean(y, axis=-1, keepdims=True)
    var = jnp.mean(jnp.square(y - mu), axis=-1, keepdims=True)
    yn = ((y - mu) * lax.rsqrt(var + GN_EPS)).reshape(B, T, A_W) * lp['rk_ln_g'] + lp['rk_ln_b']
    g = pmm3(jax.nn.sigmoid(pg.astype(F32)), lp['rk_g_up'])
    return (yn * g).astype(pr.dtype), jnp.stack(finals, axis=1)


def short_conv(x, w):
    C = x.shape[-1]
    y = lax.conv_general_dilated(x, w[:, None, :].astype(x.dtype), window_strides=(1,),
                                 padding=[(B_CONV // 2, B_CONV // 2)],
                                 dimension_numbers=('NWC', 'WIO', 'NWC'), feature_group_count=C)
    return jax.nn.silu(y)


def gdn_chunk(q, k, v, g, beta, s0):
    B, T, H, _ = q.shape
    DV = v.shape[-1]
    n = T // B_CHUNK

    def chunks(t):
        t = t.reshape((B, n, B_CHUNK, H) + t.shape[3:])
        return jnp.moveaxis(t, 3, 1)

    q, k, v, g, beta = (chunks(t) for t in (q, k, v, g, beta))
    gc = jnp.cumsum(g, axis=-1)
    idx = jnp.arange(B_CHUNK)
    lower = idx[:, None] >= idx[None, :]
    strict = idx[:, None] > idx[None, :]
    decay = jnp.exp(jnp.where(lower, gc[..., :, None] - gc[..., None, :], -jnp.inf))
    kb = k * beta[..., None]
    lmat = jnp.where(strict, jnp.einsum('bhncd,bhnsd->bhncs', kb, k) * decay, 0.0)
    tmat = lmat + jnp.eye(B_CHUNK, dtype=lmat.dtype)
    u = lax.linalg.triangular_solve(tmat, v * beta[..., None], left_side=True, lower=True, unit_diagonal=True)
    wk = lax.linalg.triangular_solve(tmat, kb * jnp.exp(gc)[..., None], left_side=True, lower=True, unit_diagonal=True)
    aqk = jnp.where(lower, jnp.einsum('bhncd,bhnsd->bhncs', q, k) * decay, 0.0)

    def step(S, inp):
        q_i, k_i, u_i, w_i, gc_i, a_i = inp
        v_new = u_i - jnp.einsum('bhck,bhkv->bhcv', w_i, S)
        o = jnp.einsum('bhck,bhkv->bhcv', q_i * jnp.exp(gc_i)[..., None], S) + jnp.einsum('bhcs,bhsv->bhcv', a_i, v_new)
        g_last = gc_i[..., -1]
        S = S * jnp.exp(g_last)[..., None, None] + jnp.einsum(
            'bhck,bhcv->bhkv', k_i * jnp.exp(g_last[..., None] - gc_i)[..., None], v_new)
        return S, o

    xs = tuple(jnp.moveaxis(t, 2, 0) for t in (q, k, u, wk, gc, aqk))
    s_fin, o = lax.scan(step, s0, xs)
    o = jnp.moveaxis(o, 0, 2).reshape(B, H, T, DV)
    return jnp.swapaxes(o, 1, 2), s_fin


def gdn_branch(pq, pk, pv, p_alpha, p_beta, pz, lp, s0):
    B, T, _ = pq.shape
    qkv = short_conv(jnp.concatenate([pq, pk, pv], axis=-1), lp['gd_conv'])
    q, k, v = jnp.split(qkv, 3, axis=-1)
    q = l2_normalize(q.reshape(B, T, B_HEADS, B_HD)) * (B_HD ** -0.5)
    k = l2_normalize(k.reshape(B, T, B_HEADS, B_HD))
    v = v.reshape(B, T, B_HEADS, B_HD).astype(F32)
    alpha = p_alpha.astype(F32).reshape(B, T, 2, B_HEADS)
    beta_in = p_beta.astype(F32).reshape(B, T, 2, B_HEADS)

    def rev(t):
        return jnp.flip(t, axis=1)

    o = jnp.zeros_like(v)
    finals = []
    for d in range(2):
        g = -jnp.exp(lp['gd_a_log'][d].astype(F32)) * jax.nn.softplus(alpha[:, :, d] + lp['gd_dt_bias'][d])
        beta = jax.nn.sigmoid(beta_in[:, :, d])
        if d == 0:
            o_d, s_d = gdn_chunk(q, k, v, g, beta, s0[:, d])
        else:
            o_d, s_d = gdn_chunk(rev(q), rev(k), rev(v), rev(g), rev(beta), s0[:, d])
            o_d = rev(o_d)
        o = o + o_d
        finals.append(s_d)
    z = jax.nn.silu(pz.reshape(B, T, B_HEADS, B_HD))
    y = rms_norm(o, lp['gd_norm_g']) * z
    return y.reshape(B, T, B_W).astype(pq.dtype), jnp.stack(finals, axis=1)


def attention_branch(pq, pk, pv, lp, ctx_kv):
    B, T, _ = pq.shape
    q = rms_norm(pq.reshape(B, T, C_HEADS, C_HD), lp['at_q_g'])
    k = rms_norm(pk.reshape(B, T, C_KV, C_HD), lp['at_k_g'])
    v = pv.reshape(B, T, C_KV, C_HD)
    if ctx_kv is None:
        o = blocked_attention(q, k, v)
    else:
        keys = jnp.concatenate([ctx_kv[0].astype(k.dtype), axial_rope(k)], axis=1)
        vals = jnp.concatenate([ctx_kv[1].astype(v.dtype), v], axis=1)
        o = blocked_attention(axial_rope(q), keys, vals)
    return o.reshape(B, T, C_W), k, v


PEER_ROUTE_TM = 256
PEER_TM = 512
PEER_I1_PER_STEP = 8
PEER_TE = PEER_I1_PER_STEP * P_NKEYS
PEER_LANES = 128
N_AUX = 4
NEG_INF = float('-inf')


def _top16_rows(s):
    rows = []
    cur = s
    for _ in range(P_TOPK):
        m = jnp.max(cur, axis=0, keepdims=True)
        rows.append(m)
        cur = jnp.where(cur == m, NEG_INF, cur)
    return rows


def _rows_to_mat(rows):
    n = len(rows)
    tm = rows[0].shape[1]
    rid = lax.broadcasted_iota(jnp.int32, (n, tm), 0)
    mat = jnp.zeros((n, tm), F32)
    for r, row in enumerate(rows):
        mat = jnp.where(rid == r, row, mat)
    return mat


def _peer_route_kernel(x_ref, wq_ref, keys_ref, st_ref, aux_ref, xb_ref):
    xb = x_ref[...].astype(BF16)
    xb_ref[...] = xb
    q = jnp.dot(xb, wq_ref[...], preferred_element_type=F32).astype(BF16)
    for h in range(P_HEADS):
        tops = []
        for p in range(2):
            c = (h * 2 + p) * P_DHALF
            s = lax.dot_general(keys_ref[h * 2 + p], q[:, c:c + P_DHALF], (((1,), (1,)), ((), ())),
                                preferred_element_type=F32)
            st_ref[h * 2 + p] = s
            tops.append(_top16_rows(s))
        t1_rows, t2_rows = tops
        t2 = _rows_to_mat(t2_rows)
        cands = [t1_rows[a] + t2 for a in range(P_TOPK)]
        m0 = t1_rows[0] + t2_rows[0]
        z = jnp.zeros_like(m0)
        tau = m0
        for _ in range(P_TOPK):
            m = cands[0]
            for cnd in cands[1:]:
                m = jnp.maximum(m, cnd)
            m = jnp.max(m, axis=0, keepdims=True)
            z = z + jnp.exp(m - m0)
            tau = m
            cands = [jnp.where(cnd == m, NEG_INF, cnd) for cnd in cands]
        aux_ref[h * N_AUX + 0:h * N_AUX + 1, :] = tau
        aux_ref[h * N_AUX + 1:h * N_AUX + 2, :] = t1_rows[0]
        aux_ref[h * N_AUX + 2:h * N_AUX + 3, :] = t2_rows[0]
        aux_ref[h * N_AUX + 3:h * N_AUX + 4, :] = 1.0 / z


def _peer_mix_kernel(xb_ref, st_ref, aux_ref, u_ref, vt_ref, o_ref, e1_ref, e2_ref, acc_ref):
    e = pl.program_id(1)
    tm = xb_ref.shape[0]
    te = u_ref.shape[0]

    @pl.when(e == 0)
    def _():
        acc_ref[...] = jnp.zeros_like(acc_ref)
        for h in range(P_HEADS):
            m1 = aux_ref[h * N_AUX + 1:h * N_AUX + 2, :]
            m2 = aux_ref[h * N_AUX + 2:h * N_AUX + 3, :]
            iz = aux_ref[h * N_AUX + 3:h * N_AUX + 4, :]
            e1_ref[h] = jnp.exp(st_ref[2 * h] - m1) * iz
            e2_ref[h] = jnp.exp(st_ref[2 * h + 1] - m2)

    hid = lax.dot_general(u_ref[...], xb_ref[...], (((1,), (1,)), ((), ())),
                          preferred_element_type=F32)
    act = jax.nn.gelu(hid)

    i1_rows = pl.ds(pl.multiple_of(e * PEER_I1_PER_STEP, PEER_I1_PER_STEP), PEER_I1_PER_STEP)
    cols = []
    for c0 in range(0, tm, PEER_LANES):
        lanes = pl.ds(c0, PEER_LANES)
        gs = []
        for r in range(PEER_I1_PER_STEP):
            g = jnp.zeros((P_NKEYS, PEER_LANES), F32)
            for h in range(P_HEADS):
                tau = aux_ref[h * N_AUX:h * N_AUX + 1, lanes]
                s1 = st_ref[2 * h, i1_rows, lanes][r:r + 1, :]
                e1 = e1_ref[h, i1_rows, lanes][r:r + 1, :]
                ssum = s1 + st_ref[2 * h + 1, :, lanes]
                g = jnp.where(ssum >= tau, g + e1 * e2_ref[h, :, lanes], g)
            gs.append(g)
        cols.append(jnp.concatenate(gs, axis=0))
    gate = jnp.concatenate(cols, axis=1)
    w = (act * gate).astype(BF16)
    acc_ref[...] += jnp.dot(vt_ref[...], w, preferred_element_type=F32)

    @pl.when(e == pl.num_programs(1) - 1)
    def _():
        o_ref[...] = acc_ref[...].T


def peer(h, wq, sub_keys, u_tab, v_tab):
    B, T, D = h.shape
    N = B * T
    x = h.reshape(N, D)
    n_hp = P_HEADS * 2
    n_exp = u_tab.shape[0]
    keys_b = sub_keys.reshape(n_hp, P_NKEYS, P_DHALF).astype(BF16)
    st, aux, xb = pl.pallas_call(
        _peer_route_kernel,
        grid=(N // PEER_ROUTE_TM,),
        in_specs=[pl.BlockSpec((PEER_ROUTE_TM, D), lambda i: (i, 0)),
                  pl.BlockSpec((D, P_HEADS * P_DKEY), lambda i: (0, 0)),
                  pl.BlockSpec((n_hp, P_NKEYS, P_DHALF), lambda i: (0, 0, 0))],
        out_specs=[pl.BlockSpec((n_hp, P_NKEYS, PEER_ROUTE_TM), lambda i: (0, 0, i)),
                   pl.BlockSpec((P_HEADS * N_AUX, PEER_ROUTE_TM), lambda i: (0, i)),
                   pl.BlockSpec((PEER_ROUTE_TM, D), lambda i: (i, 0))],
        out_shape=[jax.ShapeDtypeStruct((n_hp, P_NKEYS, N), F32),
                   jax.ShapeDtypeStruct((P_HEADS * N_AUX, N), F32),
                   jax.ShapeDtypeStruct((N, D), BF16)],
        compiler_params=pltpu.CompilerParams(
            dimension_semantics=("arbitrary",), vmem_limit_bytes=V7X_VMEM_LIMIT_BYTES),
        name="peer_route",
    )(x, wq.astype(BF16), keys_b)

    out = pl.pallas_call(
        _peer_mix_kernel,
        grid=(N // PEER_TM, n_exp // PEER_TE),
        in_specs=[pl.BlockSpec((PEER_TM, D), lambda j, e: (j, 0)),
                  pl.BlockSpec((n_hp, P_NKEYS, PEER_TM), lambda j, e: (0, 0, j)),
                  pl.BlockSpec((P_HEADS * N_AUX, PEER_TM), lambda j, e: (0, j)),
                  pl.BlockSpec((PEER_TE, D), lambda j, e: (e, 0)),
                  pl.BlockSpec((D, PEER_TE), lambda j, e: (0, e))],
        out_specs=pl.BlockSpec((PEER_TM, D), lambda j, e: (j, 0)),
        out_shape=jax.ShapeDtypeStruct((N, D), F32),
        scratch_shapes=[pltpu.VMEM((P_HEADS, P_NKEYS, PEER_TM), F32),
                        pltpu.VMEM((P_HEADS, P_NKEYS, PEER_TM), F32),
                        pltpu.VMEM((D, PEER_TM), F32)],
        compiler_params=pltpu.CompilerParams(
            dimension_semantics=("arbitrary", "arbitrary"), vmem_limit_bytes=V7X_VMEM_LIMIT_BYTES),
        name="peer_mix",
    )(xb, st, aux, u_tab.astype(BF16), v_tab.astype(BF16).T)
    return out.reshape(B, T, D)


def trunk_layer(x, cvec, lp, ctx):
    B, T, _ = x.shape
    mod = jax.nn.silu(cvec) @ lp['w_mod'] + lp['b_mod']
    sh1, sc1, g1, sh2, sc2, g2 = [m[:, None, :] for m in jnp.split(mod, N_MOD, axis=-1)]
    h = rms_norm(x, lp['norm1_g']) * (1 + sc1) + sh1
    (ar, ak, av, aw, aa, ag, bq, bk, bv, b_alpha, b_beta, bz, cq, ck, cv, gl) = split_cols(
        pmm3(h, lp['w_in']), IN_SPLITS)
    if ctx is None:
        s_rwkv0 = jnp.zeros((B, 2, A_HEADS, A_HD, A_HD), F32)
        s_gdn0 = jnp.zeros((B, 2, B_HEADS, B_HD, B_HD), F32)
        ctx_kv = None
    else:
        s_rwkv0 = ctx['rwkv'].astype(F32)
        s_gdn0 = ctx['gdn'].astype(F32)
        ctx_kv = (ctx['k'], ctx['v'])
    ya, s_rwkv = rwkv7_branch(ar, ak, av, aw, aa, ag, lp, s_rwkv0)
    yb, s_gdn = gdn_branch(bq, bk, bv, b_alpha, b_beta, bz, lp, s_gdn0)
    yc, k_c, v_c = attention_branch(cq, ck, cv, lp, ctx_kv)
    gate = jax.nn.sigmoid(gl.reshape(B, T, N_BRANCH, D_MODEL))
    merged = (gate[:, :, 0] * pmm3(ya, lp['w_br_a']) + gate[:, :, 1] * pmm3(yb, lp['w_br_b'])
              + gate[:, :, 2] * pmm3(yc, lp['w_br_c']))
    x = x + g1 * pmm3(merged, lp['w_out'])
    h2 = rms_norm(x, lp['norm2_g']) * (1 + sc2) + sh2
    x = x + g2 * peer(h2, lp['pk_wq'], lp['pk_keys'], lp['pk_u'], lp['pk_v'])
    return x, k_c, v_c, s_rwkv, s_gdn


_PARAM_NAMES = ('w_mod', 'b_mod', 'norm1_g', 'w_in',
                'rk_w_up', 'rk_w0', 'rk_a_up', 'rk_a0', 'rk_g_up', 'rk_kk', 'rk_ka', 'rk_rk', 'rk_ln_g', 'rk_ln_b',
                'gd_conv', 'gd_a_log', 'gd_dt_bias', 'gd_norm_g',
                'at_q_g', 'at_k_g',
                'w_br_a', 'w_br_b', 'w_br_c', 'w_out',
                'norm2_g', 'pk_wq', 'pk_keys', 'pk_u', 'pk_v')


def kernel(x_prompt, x_sample, cache_attn_k, cache_attn_v, state_rwkv, state_gdn, c, c_ctx, w_mod, b_mod, norm1_g, w_in, rk_w_up, rk_w0, rk_a_up, rk_a0, rk_g_up, rk_kk, rk_ka, rk_rk, rk_ln_g, rk_ln_b, gd_conv, gd_a_log, gd_dt_bias, gd_norm_g, at_q_g, at_k_g, w_br_a, w_br_b, w_br_c, w_out, norm2_g, pk_wq, pk_keys, pk_u, pk_v, final_g):
    stacked = dict(zip(_PARAM_NAMES, (
        w_mod, b_mod, norm1_g, w_in,
        rk_w_up, rk_w0, rk_a_up, rk_a0, rk_g_up, rk_kk, rk_ka, rk_rk, rk_ln_g, rk_ln_b,
        gd_conv, gd_a_log, gd_dt_bias, gd_norm_g,
        at_q_g, at_k_g,
        w_br_a, w_br_b, w_br_c, w_out,
        norm2_g, pk_wq, pk_keys, pk_u, pk_v)))
    xp = x_prompt
    xs = x_sample
    k_list, v_list, sr_list, sg_list = [], [], [], []
    for l in range(DEPTH):
        lp = {name: arr[l] for name, arr in stacked.items()}
        xp, k_c, v_c, s_r, s_g = trunk_layer(xp, c_ctx[None, :], lp, None)
        k_list.append(k_c)
        v_list.append(v_c)
        sr_list.append(s_r)
        sg_list.append(s_g)
        ctx = {'k': cache_attn_k[:, l], 'v': cache_attn_v[:, l], 'rwkv': state_rwkv[:, l], 'gdn': state_gdn[:, l]}
        xs = trunk_layer(xs, c, lp, ctx)[0]
    y_prompt = rms_norm(xp, final_g)
    y_sample = rms_norm(xs, final_g)
    return (y_prompt, y_sample, jnp.stack(k_list, axis=1), jnp.stack(v_list, axis=1),
            jnp.stack(sr_list, axis=1).astype(x_prompt.dtype), jnp.stack(sg_list, axis=1).astype(x_prompt.dtype))
```

```python
import functools
import math

import jax
import jax.numpy as jnp
import numpy as np
from jax import lax
from jax.experimental import pallas as pl
from jax.experimental.pallas import tpu as pltpu

F32 = jnp.float32
BF16 = jnp.bfloat16

D_MODEL = 1024
DEPTH = 2
GRID_W = 64
N_MOD = 6
EPS = 1e-6
GN_EPS = 64e-5
A_HEADS = 8
A_HD = 64
A_W = A_HEADS * A_HD
A_LORA_W = 64
A_LORA_A = 64
A_LORA_G = 128
A_DECAY_SCALE = 0.606531
B_HEADS = 4
B_HD = 128
B_W = B_HEADS * B_HD
B_CONV = 5
B_CHUNK = 64
C_HEADS = 8
C_KV = 2
C_HD = 64
C_W = C_HEADS * C_HD
C_KVW = C_KV * C_HD
C_QBLOCK = 128
ROPE_BASE = 10000.0
N_BRANCH = 3
IN_SPLITS = (A_W, A_W, A_W, 2 * A_LORA_W, 2 * A_LORA_A, A_LORA_G,
             B_W, B_W, B_W, 2 * B_HEADS, 2 * B_HEADS, B_W,
             C_W, C_KVW, C_KVW, N_BRANCH * D_MODEL)
N_IN = sum(IN_SPLITS)
P_HEADS = 8
P_NKEYS = 128
P_TOPK = 16
P_DKEY = 256
P_DHALF = P_DKEY // 2
P_BLOCK = 128

V7X_VMEM_LIMIT_BYTES = 48 * 1024 * 1024
LANE = 128


def _mm_kernel(x_ref, w_ref, o_ref):
    o_ref[...] = jnp.dot(x_ref[...].astype(BF16), w_ref[...], preferred_element_type=F32)


def _pick_tile(n, candidates):
    for c in candidates:
        if n % c == 0:
            return c
    return n


def pmm(x, w):
    M, K = x.shape
    N = w.shape[1]
    n_pad = (-N) % LANE
    wb = w.astype(BF16)
    if n_pad:
        wb = jnp.pad(wb, ((0, 0), (0, n_pad)))
    Np = N + n_pad
    tm = _pick_tile(M, (1024, 512, 256, 128, 64, 32, 16, 8))
    tn = _pick_tile(Np, (512, 256, 128))
    out = pl.pallas_call(
        _mm_kernel,
        grid=(M // tm, Np // tn),
        in_specs=[pl.BlockSpec((tm, K), lambda i, j: (i, 0)),
                  pl.BlockSpec((K, tn), lambda i, j: (0, j))],
        out_specs=pl.BlockSpec((tm, tn), lambda i, j: (i, j)),
        out_shape=jax.ShapeDtypeStruct((M, Np), F32),
        compiler_params=pltpu.CompilerParams(
            dimension_semantics=("arbitrary", "arbitrary"),
            vmem_limit_bytes=V7X_VMEM_LIMIT_BYTES),
        name="pmm",
    )(x, wb)
    return out[:, :N] if n_pad else out


def pmm3(x, w):
    B, T, K = x.shape
    return pmm(x.reshape(B * T, K), w).reshape(B, T, w.shape[1])


def rms_norm(x, g):
    x32 = x.astype(F32)
    y = x32 * lax.rsqrt(jnp.mean(x32 * x32, axis=-1, keepdims=True) + EPS)
    return (y * g.astype(F32)).astype(x.dtype)


def l2_normalize(x):
    x32 = x.astype(F32)
    return x32 * lax.rsqrt(jnp.sum(x32 * x32, axis=-1, keepdims=True) + EPS)


def split_cols(p, sizes):
    cuts = np.cumsum(np.asarray(sizes))[:-1].tolist()
    return jnp.split(p, cuts, axis=-1)


def axial_rope(x):
    B, T, H, D = x.shape
    n_rows = T // GRID_W
    n_freq = D // 4
    rows = jnp.repeat(jnp.arange(n_rows, dtype=F32), GRID_W)
    cols = jnp.tile(jnp.arange(GRID_W, dtype=F32), n_rows)
    inv = ROPE_BASE ** (-jnp.arange(n_freq, dtype=F32) / n_freq)
    ang = jnp.stack([rows[:, None] * inv, cols[:, None] * inv], axis=1)
    cos = jnp.cos(ang)[None, :, None]
    sin = jnp.sin(ang)[None, :, None]
    xr = x.astype(F32).reshape(B, T, H, 2, 2, n_freq)
    x1 = xr[..., 0, :]
    x2 = xr[..., 1, :]
    out = jnp.stack([x1 * cos - x2 * sin, x2 * cos + x1 * sin], axis=-2)
    return out.reshape(B, T, H, D).astype(x.dtype)


def blocked_attention(q, k, v):
    B, T, H, D = q.shape
    nb = T // C_QBLOCK
    grp = H // C_KV
    qb = jnp.moveaxis(q.reshape(B, nb, C_QBLOCK, C_KV, grp, D), 1, 0)
    scale = D ** -0.5

    def one_block(qi):
        s = jnp.einsum('bqkgd,bskd->bkgqs', qi, k).astype(F32) * scale
        p = jax.nn.softmax(s, axis=-1).astype(v.dtype)
        return jnp.einsum('bkgqs,bskd->bqkgd', p, v)

    o = lax.map(one_block, qb)
    return jnp.moveaxis(o, 0, 1).reshape(B, T, H, D)


RWKV_CHUNK = 128
HIGHEST = lax.Precision.HIGHEST


def _mm(a, b):
    return jnp.dot(a.astype(BF16), b.astype(BF16), preferred_element_type=F32)


def _mm_nt(a, b):
    return lax.dot_general(a.astype(BF16), b.astype(BF16), (((1,), (1,)), ((), ())), preferred_element_type=F32)


def _mm_tn(a, b):
    return lax.dot_general(a.astype(BF16), b.astype(BF16), (((0,), (0,)), ((), ())), preferred_element_type=F32)


def _split_bf16(a):
    hi = a.astype(BF16)
    return hi, (a - hi.astype(F32)).astype(BF16)


def _mm_x3(a, b):
    a_hi, a_lo = _split_bf16(a)
    b_hi, b_lo = _split_bf16(b)
    return (jnp.dot(a_hi, b_hi, preferred_element_type=F32)
            + (jnp.dot(a_hi, b_lo, preferred_element_type=F32) + jnp.dot(a_lo, b_hi, preferred_element_type=F32)))


TRI_BASE = 8


def _unit_tri_inverse(l_mat, mm=_mm):
    n = l_mat.shape[0]
    row = lax.broadcasted_iota(jnp.int32, (n, n), 0)
    col = lax.broadcasted_iota(jnp.int32, (n, n), 1)

    def same_block(b):
        shift = b.bit_length() - 1
        return jnp.right_shift(row, shift) == jnp.right_shift(col, shift)

    a = jnp.where(same_block(TRI_BASE), l_mat, 0.0)
    x = jnp.where(row == col, 1.0, 0.0) + a
    p = a
    span = 2
    while span < TRI_BASE:
        p = mm(p, p)
        x = x + mm(x, p)
        span *= 2
    b = TRI_BASE
    while b < n:
        off = jnp.where(same_block(2 * b) & jnp.logical_not(same_block(b)), l_mat, 0.0)
        x = x + mm(mm(x, off), x)
        b *= 2
    return x


def _rwkv_chunk_kernel(incl_ref, strict_ref, ka_ref, r_ref, k_ref, v_ref, kk_ref, lw_ref, a_ref, s0_ref,
                       y_ref, sfin_ref, s_ref):
    c = pl.program_id(2)

    @pl.when(c == 0)
    def _():
        s_ref[...] = s0_ref[0, 0]

    incl = incl_ref[0]
    strict = strict_ref[0]
    for h in range(r_ref.shape[1]):
        r = r_ref[0, h]
        k = k_ref[0, h]
        v = v_ref[0, h]
        kk = kk_ref[0, h]
        lw = lw_ref[0, 0, h]
        a = a_ref[0, 0, h]
        s_old = s_ref[h]
        k_d = k * (1.0 + (a - 1.0) * ka_ref[h])
        bb = kk * a
        lam = jnp.dot(incl, lw, precision=HIGHEST, preferred_element_type=F32)
        lam_end = jnp.sum(lw, axis=0, keepdims=True)
        e_neg = jnp.exp(-lam)
        a_t = -kk * jnp.exp(lam - lw)
        r_t = r * jnp.exp(lam)
        ar = jnp.concatenate([a_t, r_t], axis=0)
        bk = jnp.concatenate([bb * e_neg, k_d * e_neg], axis=0)
        p = _mm_nt(ar, s_old)
        m = _mm_nt(ar, bk)
        cs = a_t.shape[0]
        l_ab = m[:cs, :cs] * strict
        l_ak = m[:cs, cs:] * strict
        n_rb = m[cs:, :cs] * incl
        n_rk = m[cs:, cs:] * incl
        u = _mm(_unit_tri_inverse(l_ab), p[:cs] + _mm(l_ak, v))
        y_ref[0, 0, h] = p[cs:] + _mm(n_rb, u) + _mm(n_rk, v)
        e_end = jnp.exp(lam_end - lam)
        s_ref[h] = s_old * jnp.exp(lam_end) + _mm_tn(u, bb * e_end) + _mm_tn(v, k_d * e_end)

    @pl.when(c == pl.num_programs(2) - 1)
    def _():
        sfin_ref[0, 0] = s_ref[...]


def rwkv_scan(r, k, v, kk, logw, a, k_a, s0):
    B, H, T, N = r.shape
    C = RWKV_CHUNK
    nc = T // C
    idx = jnp.arange(C)
    incl = jnp.stack([idx[:, None] >= idx[None, :], idx[:, None] <= idx[None, :]]).astype(F32)
    strict = jnp.stack([idx[:, None] > idx[None, :], idx[:, None] < idx[None, :]]).astype(F32)

    def chunk(d, c):
        return c + d * (nc - 1 - 2 * c)

    seq_spec = pl.BlockSpec((1, H, C, N), lambda b, d, c: (b, 0, chunk(d, c), 0))
    dir_spec = pl.BlockSpec((1, 1, H, C, N), lambda b, d, c: (d, b, 0, chunk(d, c), 0))
    state_spec = pl.BlockSpec((1, 1, H, N, N), lambda b, d, c: (b, d, 0, 0, 0))
    return pl.pallas_call(
        _rwkv_chunk_kernel,
        grid=(B, 2, nc),
        in_specs=[pl.BlockSpec((1, C, C), lambda b, d, c: (d, 0, 0)),
                  pl.BlockSpec((1, C, C), lambda b, d, c: (d, 0, 0)),
                  pl.BlockSpec((H, 1, N), lambda b, d, c: (0, 0, 0)),
                  seq_spec, seq_spec, seq_spec, seq_spec, dir_spec, dir_spec, state_spec],
        out_specs=[dir_spec, state_spec],
        out_shape=[jax.ShapeDtypeStruct((2, B, H, T, N), F32),
                   jax.ShapeDtypeStruct((B, 2, H, N, N), F32)],
        scratch_shapes=[pltpu.VMEM((H, N, N), F32)],
        compiler_params=pltpu.CompilerParams(
            dimension_semantics=("arbitrary", "arbitrary", "arbitrary"), vmem_limit_bytes=V7X_VMEM_LIMIT_BYTES),
        name="rwkv_chunk",
    )(incl, strict, k_a, r, k, v, kk, logw, a, s0)


def rwkv7_branch(pr, pk, pv, pw, pa, pg, lp, s0):
    B, T, _ = pr.shape

    def heads(t):
        return t.astype(F32).reshape(B, T, A_HEADS, A_HD)

    def bh(t):
        return jnp.swapaxes(t, -3, -2)

    r, k, v = heads(pr), heads(pk), heads(pv)
    k_k = lp['rk_kk'].astype(F32).reshape(A_HEADS, A_HD)
    k_a = lp['rk_ka'].astype(F32).reshape(A_HEADS, A_HD)
    r_k = lp['rk_rk'].astype(F32)
    kk = l2_normalize(k * k_k)
    pw = pw.astype(F32).reshape(B, T, 2, A_LORA_W)
    pa = pa.astype(F32).reshape(B, T, 2, A_LORA_A)
    logw = jnp.stack([heads(-A_DECAY_SCALE * jax.nn.sigmoid(lp['rk_w0'][d] + pmm3(jnp.tanh(pw[:, :, d]), lp['rk_w_up'][d])))
                      for d in range(2)])
    a = jnp.stack([heads(jax.nn.sigmoid(lp['rk_a0'][d] + pmm3(pa[:, :, d], lp['rk_a_up'][d]))) for d in range(2)])
    y_dir, s_fin = rwkv_scan(bh(r), bh(k), bh(v), bh(kk), bh(logw), bh(a), k_a[:, None, :], s0)
    y = bh(y_dir[0] + y_dir[1])
    for d in range(2):
        k_d = k * (1.0 + (a[d] - 1.0) * k_a)
        y = y + jnp.sum(r * k_d * r_k, axis=-1, keepdims=True) * v
    mu = jnp.mean(y, axis=-1, keepdims=True)
    var = jnp.mean(jnp.square(y - mu), axis=-1, keepdims=True)
    yn = ((y - mu) * lax.rsqrt(var + GN_EPS)).reshape(B, T, A_W) * lp['rk_ln_g'] + lp['rk_ln_b']
    g = pmm3(jax.nn.sigmoid(pg.astype(F32)), lp['rk_g_up'])
    return (yn * g).astype(pr.dtype), s_fin


def short_conv(x, w):
    C = x.shape[-1]
    y = lax.conv_general_dilated(x, w[:, None, :].astype(x.dtype), window_strides=(1,),
                                 padding=[(B_CONV // 2, B_CONV // 2)],
                                 dimension_numbers=('NWC', 'WIO', 'NWC'), feature_group_count=C)
    return jax.nn.silu(y)


GDN_CHUNK = 128


def _gdn_chunk_kernel(incl_ref, strict_ref, q_ref, k_ref, v_ref, g_ref, beta_ref, s0_ref, o_ref, sfin_ref, s_ref):
    c = pl.program_id(2)

    @pl.when(c == 0)
    def _():
        s_ref[...] = s0_ref[0, 0]

    incl = incl_ref[0]
    strict = strict_ref[0]
    cs = incl.shape[0]
    eye = (lax.broadcasted_iota(jnp.int32, (cs, cs), 0) == lax.broadcasted_iota(jnp.int32, (cs, cs), 1)).astype(F32)
    hd = s_ref.shape[1]
    for h in range(s_ref.shape[0]):
        lanes = slice(h * hd, (h + 1) * hd)
        q = q_ref[0, :, lanes]
        k = k_ref[0, :, lanes]
        v = v_ref[0, :, lanes]
        g_row = g_ref[0, 0, 0, h:h + 1, :]
        beta_row = beta_ref[0, 0, 0, h:h + 1, :]
        gc_row = lax.dot_general(g_row, incl, (((1,), (1,)), ((), ())), precision=HIGHEST,
                                 preferred_element_type=F32)
        gc_col = jnp.sum(eye * gc_row, axis=1, keepdims=True)
        beta_col = jnp.sum(eye * beta_row, axis=1, keepdims=True)
        g_end = jnp.sum(g_row, axis=1, keepdims=True)
        decay = jnp.exp(jnp.where(incl > 0, gc_col - gc_row, NEG_INF))
        kb = k * beta_col
        lmat = strict * _mm_nt(kb, k) * decay
        tinv = _unit_tri_inverse(-lmat, mm=_mm_x3)
        u = _mm(tinv, v * beta_col)
        wk = _mm(tinv, kb * jnp.exp(gc_col))
        aqk = incl * _mm_nt(q, k) * decay
        s_old = s_ref[h]
        v_new = u - _mm(wk, s_old)
        o_ref[0, 0, :, lanes] = _mm(q * jnp.exp(gc_col), s_old) + _mm(aqk, v_new)
        s_ref[h] = s_old * jnp.exp(g_end) + _mm_tn(k * jnp.exp(g_end - gc_col), v_new)

    @pl.when(c == pl.num_programs(2) - 1)
    def _():
        sfin_ref[0, 0] = s_ref[...]


def gdn_scan(q, k, v, g, beta, s0):
    B, T, W = q.shape
    H, D = B_HEADS, B_HD
    C = GDN_CHUNK
    nc = T // C
    idx = jnp.arange(C)
    incl = jnp.stack([idx[:, None] >= idx[None, :], idx[:, None] <= idx[None, :]]).astype(F32)
    strict = jnp.stack([idx[:, None] > idx[None, :], idx[:, None] < idx[None, :]]).astype(F32)

    def rows(t):
        return jnp.swapaxes(t.reshape(2, B, nc, C, H), -1, -2)

    def chunk(d, c):
        return c + d * (nc - 1 - 2 * c)

    seq_spec = pl.BlockSpec((1, C, W), lambda b, d, c: (b, chunk(d, c), 0))
    row_spec = pl.BlockSpec((1, 1, 1, H, C), lambda b, d, c: (d, b, chunk(d, c), 0, 0))
    state_spec = pl.BlockSpec((1, 1, H, D, D), lambda b, d, c: (b, d, 0, 0, 0))
    return pl.pallas_call(
        _gdn_chunk_kernel,
        grid=(B, 2, nc),
        in_specs=[pl.BlockSpec((1, C, C), lambda b, d, c: (d, 0, 0)),
                  pl.BlockSpec((1, C, C), lambda b, d, c: (d, 0, 0)),
                  seq_spec, seq_spec, seq_spec, row_spec, row_spec, state_spec],
        out_specs=[pl.BlockSpec((1, 1, C, W), lambda b, d, c: (d, b, chunk(d, c), 0)), state_spec],
        out_shape=[jax.ShapeDtypeStruct((2, B, T, W), F32),
                   jax.ShapeDtypeStruct((B, 2, H, D, D), F32)],
        scratch_shapes=[pltpu.VMEM((H, D, D), F32)],
        compiler_params=pltpu.CompilerParams(
            dimension_semantics=("arbitrary", "arbitrary", "arbitrary"), vmem_limit_bytes=V7X_VMEM_LIMIT_BYTES),
        name="gdn_chunk",
    )(incl, strict, q, k, v, rows(g), rows(beta), s0)


def gdn_branch(pq, pk, pv, p_alpha, p_beta, pz, lp, s0):
    B, T, _ = pq.shape
    qkv = short_conv(jnp.concatenate([pq, pk, pv], axis=-1), lp['gd_conv'])
    q, k, v = jnp.split(qkv, 3, axis=-1)
    q = (l2_normalize(q.reshape(B, T, B_HEADS, B_HD)) * (B_HD ** -0.5)).reshape(B, T, B_W)
    k = l2_normalize(k.reshape(B, T, B_HEADS, B_HD)).reshape(B, T, B_W)
    v = v.astype(F32)
    alpha = p_alpha.astype(F32).reshape(B, T, 2, B_HEADS)
    beta_in = p_beta.astype(F32).reshape(B, T, 2, B_HEADS)
    g = jnp.stack([-jnp.exp(lp['gd_a_log'][d].astype(F32)) * jax.nn.softplus(alpha[:, :, d] + lp['gd_dt_bias'][d])
                   for d in range(2)])
    beta = jnp.stack([jax.nn.sigmoid(beta_in[:, :, d]) for d in range(2)])
    o_dir, s_fin = gdn_scan(q, k, v, g, beta, s0)
    o = (o_dir[0] + o_dir[1]).reshape(B, T, B_HEADS, B_HD)
    z = jax.nn.silu(pz.reshape(B, T, B_HEADS, B_HD))
    y = rms_norm(o, lp['gd_norm_g']) * z
    return y.reshape(B, T, B_W).astype(pq.dtype), s_fin


def attention_branch(pq, pk, pv, lp, ctx_kv):
    B, T, _ = pq.shape
    q = rms_norm(pq.reshape(B, T, C_HEADS, C_HD), lp['at_q_g'])
    k = rms_norm(pk.reshape(B, T, C_KV, C_HD), lp['at_k_g'])
    v = pv.reshape(B, T, C_KV, C_HD)
    if ctx_kv is None:
        o = blocked_attention(q, k, v)
    else:
        keys = jnp.concatenate([ctx_kv[0].astype(k.dtype), axial_rope(k)], axis=1)
        vals = jnp.concatenate([ctx_kv[1].astype(v.dtype), v], axis=1)
        o = blocked_attention(axial_rope(q), keys, vals)
    return o.reshape(B, T, C_W), k, v


PEER_ROUTE_TM = 256
PEER_TM = 512
PEER_I1_PER_STEP = 8
PEER_TE = PEER_I1_PER_STEP * P_NKEYS
PEER_LANES = 128
N_AUX = 4
NEG_INF = float('-inf')


def _top16_rows(s):
    rows = []
    cur = s
    for _ in range(P_TOPK):
        m = jnp.max(cur, axis=0, keepdims=True)
        rows.append(m)
        cur = jnp.where(cur == m, NEG_INF, cur)
    return rows


def _rows_to_mat(rows):
    n = len(rows)
    tm = rows[0].shape[1]
    rid = lax.broadcasted_iota(jnp.int32, (n, tm), 0)
    mat = jnp.zeros((n, tm), F32)
    for r, row in enumerate(rows):
        mat = jnp.where(rid == r, row, mat)
    return mat


def _peer_route_kernel(x_ref, wq_ref, keys_ref, st_ref, aux_ref, xb_ref):
    xb = x_ref[...].astype(BF16)
    xb_ref[...] = xb
    q = jnp.dot(xb, wq_ref[...], preferred_element_type=F32).astype(BF16)
    for h in range(P_HEADS):
        tops = []
        for p in range(2):
            c = (h * 2 + p) * P_DHALF
            s = lax.dot_general(keys_ref[h * 2 + p], q[:, c:c + P_DHALF], (((1,), (1,)), ((), ())),
                                preferred_element_type=F32)
            st_ref[h * 2 + p] = s
            tops.append(_top16_rows(s))
        t1_rows, t2_rows = tops
        t2 = _rows_to_mat(t2_rows)
        cands = [t1_rows[a] + t2 for a in range(P_TOPK)]
        m0 = t1_rows[0] + t2_rows[0]
        z = jnp.zeros_like(m0)
        tau = m0
        for _ in range(P_TOPK):
            m = cands[0]
            for cnd in cands[1:]:
                m = jnp.maximum(m, cnd)
            m = jnp.max(m, axis=0, keepdims=True)
            z = z + jnp.exp(m - m0)
            tau = m
            cands = [jnp.where(cnd == m, NEG_INF, cnd) for cnd in cands]
        aux_ref[h * N_AUX + 0:h * N_AUX + 1, :] = tau
        aux_ref[h * N_AUX + 1:h * N_AUX + 2, :] = t1_rows[0]
        aux_ref[h * N_AUX + 2:h * N_AUX + 3, :] = t2_rows[0]
        aux_ref[h * N_AUX + 3:h * N_AUX + 4, :] = 1.0 / z


def _peer_mix_kernel(xb_ref, st_ref, aux_ref, u_ref, vt_ref, o_ref, e1_ref, e2_ref, acc_ref):
    e = pl.program_id(1)
    tm = xb_ref.shape[0]
    te = u_ref.shape[0]

    @pl.when(e == 0)
    def _():
        acc_ref[...] = jnp.zeros_like(acc_ref)
        for h in range(P_HEADS):
            m1 = aux_ref[h * N_AUX + 1:h * N_AUX + 2, :]
            m2 = aux_ref[h * N_AUX + 2:h * N_AUX + 3, :]
            iz = aux_ref[h * N_AUX + 3:h * N_AUX + 4, :]
            e1_ref[h] = jnp.exp(st_ref[2 * h] - m1) * iz
            e2_ref[h] = jnp.exp(st_ref[2 * h + 1] - m2)

    hid = lax.dot_general(u_ref[...], xb_ref[...], (((1,), (1,)), ((), ())),
                          preferred_element_type=F32)
    act = jax.nn.gelu(hid)

    i1_rows = pl.ds(pl.multiple_of(e * PEER_I1_PER_STEP, PEER_I1_PER_STEP), PEER_I1_PER_STEP)
    cols = []
    for c0 in range(0, tm, PEER_LANES):
        lanes = pl.ds(c0, PEER_LANES)
        gs = []
        for r in range(PEER_I1_PER_STEP):
            g = jnp.zeros((P_NKEYS, PEER_LANES), F32)
            for h in range(P_HEADS):
                tau = aux_ref[h * N_AUX:h * N_AUX + 1, lanes]
                s1 = st_ref[2 * h, i1_rows, lanes][r:r + 1, :]
                e1 = e1_ref[h, i1_rows, lanes][r:r + 1, :]
                ssum = s1 + st_ref[2 * h + 1, :, lanes]
                g = jnp.where(ssum >= tau, g + e1 * e2_ref[h, :, lanes], g)
            gs.append(g)
        cols.append(jnp.concatenate(gs, axis=0))
    gate = jnp.concatenate(cols, axis=1)
    w = (act * gate).astype(BF16)
    acc_ref[...] += jnp.dot(vt_ref[...], w, preferred_element_type=F32)

    @pl.when(e == pl.num_programs(1) - 1)
    def _():
        o_ref[...] = acc_ref[...].T


def peer(h, wq, sub_keys, u_tab, v_tab):
    B, T, D = h.shape
    N = B * T
    x = h.reshape(N, D)
    n_hp = P_HEADS * 2
    n_exp = u_tab.shape[0]
    keys_b = sub_keys.reshape(n_hp, P_NKEYS, P_DHALF).astype(BF16)
    st, aux, xb = pl.pallas_call(
        _peer_route_kernel,
        grid=(N // PEER_ROUTE_TM,),
        in_specs=[pl.BlockSpec((PEER_ROUTE_TM, D), lambda i: (i, 0)),
                  pl.BlockSpec((D, P_HEADS * P_DKEY), lambda i: (0, 0)),
                  pl.BlockSpec((n_hp, P_NKEYS, P_DHALF), lambda i: (0, 0, 0))],
        out_specs=[pl.BlockSpec((n_hp, P_NKEYS, PEER_ROUTE_TM), lambda i: (0, 0, i)),
                   pl.BlockSpec((P_HEADS * N_AUX, PEER_ROUTE_TM), lambda i: (0, i)),
                   pl.BlockSpec((PEER_ROUTE_TM, D), lambda i: (i, 0))],
        out_shape=[jax.ShapeDtypeStruct((n_hp, P_NKEYS, N), F32),
                   jax.ShapeDtypeStruct((P_HEADS * N_AUX, N), F32),
                   jax.ShapeDtypeStruct((N, D), BF16)],
        compiler_params=pltpu.CompilerParams(
            dimension_semantics=("arbitrary",), vmem_limit_bytes=V7X_VMEM_LIMIT_BYTES),
        name="peer_route",
    )(x, wq.astype(BF16), keys_b)

    out = pl.pallas_call(
        _peer_mix_kernel,
        grid=(N // PEER_TM, n_exp // PEER_TE),
        in_specs=[pl.BlockSpec((PEER_TM, D), lambda j, e: (j, 0)),
                  pl.BlockSpec((n_hp, P_NKEYS, PEER_TM), lambda j, e: (0, 0, j)),
                  pl.BlockSpec((P_HEADS * N_AUX, PEER_TM), lambda j, e: (0, j)),
                  pl.BlockSpec((PEER_TE, D), lambda j, e: (e, 0)),
                  pl.BlockSpec((D, PEER_TE), lambda j, e: (0, e))],
        out_specs=pl.BlockSpec((PEER_TM, D), lambda j, e: (j, 0)),
        out_shape=jax.ShapeDtypeStruct((N, D), F32),
        scratch_shapes=[pltpu.VMEM((P_HEADS, P_NKEYS, PEER_TM), F32),
                        pltpu.VMEM((P_HEADS, P_NKEYS, PEER_TM), F32),
                        pltpu.VMEM((D, PEER_TM), F32)],
        compiler_params=pltpu.CompilerParams(
            dimension_semantics=("arbitrary", "arbitrary"), vmem_limit_bytes=V7X_VMEM_LIMIT_BYTES),
        name="peer_mix",
    )(xb, st, aux, u_tab.astype(BF16), v_tab.astype(BF16).T)
    return out.reshape(B, T, D)


def trunk_layer(x, cvec, lp, ctx):
    B, T, _ = x.shape
    mod = jax.nn.silu(cvec) @ lp['w_mod'] + lp['b_mod']
    sh1, sc1, g1, sh2, sc2, g2 = [m[:, None, :] for m in jnp.split(mod, N_MOD, axis=-1)]
    h = rms_norm(x, lp['norm1_g']) * (1 + sc1) + sh1
    (ar, ak, av, aw, aa, ag, bq, bk, bv, b_alpha, b_beta, bz, cq, ck, cv, gl) = split_cols(
        pmm3(h, lp['w_in']), IN_SPLITS)
    if ctx is None:
        s_rwkv0 = jnp.zeros((B, 2, A_HEADS, A_HD, A_HD), F32)
        s_gdn0 = jnp.zeros((B, 2, B_HEADS, B_HD, B_HD), F32)
        ctx_kv = None
    else:
        s_rwkv0 = ctx['rwkv'].astype(F32)
        s_gdn0 = ctx['gdn'].astype(F32)
        ctx_kv = (ctx['k'], ctx['v'])
    ya, s_rwkv = rwkv7_branch(ar, ak, av, aw, aa, ag, lp, s_rwkv0)
    yb, s_gdn = gdn_branch(bq, bk, bv, b_alpha, b_beta, bz, lp, s_gdn0)
    yc, k_c, v_c = attention_branch(cq, ck, cv, lp, ctx_kv)
    gate = jax.nn.sigmoid(gl.reshape(B, T, N_BRANCH, D_MODEL))
    merged = (gate[:, :, 0] * pmm3(ya, lp['w_br_a']) + gate[:, :, 1] * pmm3(yb, lp['w_br_b'])
              + gate[:, :, 2] * pmm3(yc, lp['w_br_c']))
    x = x + g1 * pmm3(merged, lp['w_out'])
    h2 = rms_norm(x, lp['norm2_g']) * (1 + sc2) + sh2
    x = x + g2 * peer(h2, lp['pk_wq'], lp['pk_keys'], lp['pk_u'], lp['pk_v'])
    return x, k_c, v_c, s_rwkv, s_gdn


_PARAM_NAMES = ('w_mod', 'b_mod', 'norm1_g', 'w_in',
                'rk_w_up', 'rk_w0', 'rk_a_up', 'rk_a0', 'rk_g_up', 'rk_kk', 'rk_ka', 'rk_rk', 'rk_ln_g', 'rk_ln_b',
                'gd_conv', 'gd_a_log', 'gd_dt_bias', 'gd_norm_g',
                'at_q_g', 'at_k_g',
                'w_br_a', 'w_br_b', 'w_br_c', 'w_out',
                'norm2_g', 'pk_wq', 'pk_keys', 'pk_u', 'pk_v')


def kernel(x_prompt, x_sample, cache_attn_k, cache_attn_v, state_rwkv, state_gdn, c, c_ctx, w_mod, b_mod, norm1_g, w_in, rk_w_up, rk_w0, rk_a_up, rk_a0, rk_g_up, rk_kk, rk_ka, rk_rk, rk_ln_g, rk_ln_b, gd_conv, gd_a_log, gd_dt_bias, gd_norm_g, at_q_g, at_k_g, w_br_a, w_br_b, w_br_c, w_out, norm2_g, pk_wq, pk_keys, pk_u, pk_v, final_g):
    stacked = dict(zip(_PARAM_NAMES, (
        w_mod, b_mod, norm1_g, w_in,
        rk_w_up, rk_w0, rk_a_up, rk_a0, rk_g_up, rk_kk, rk_ka, rk_rk, rk_ln_g, rk_ln_b,
        gd_conv, gd_a_log, gd_dt_bias, gd_norm_g,
        at_q_g, at_k_g,
        w_br_a, w_br_b, w_br_c, w_out,
        norm2_g, pk_wq, pk_keys, pk_u, pk_v)))
    xp = x_prompt
    xs = x_sample
    k_list, v_list, sr_list, sg_list = [], [], [], []
    for l in range(DEPTH):
        lp = {name: arr[l] for name, arr in stacked.items()}
        xp, k_c, v_c, s_r, s_g = trunk_layer(xp, c_ctx[None, :], lp, None)
        k_list.append(k_c)
        v_list.append(v_c)
        sr_list.append(s_r)
        sg_list.append(s_g)
        ctx = {'k': cache_attn_k[:, l], 'v': cache_attn_v[:, l], 'rwkv': state_rwkv[:, l], 'gdn': state_gdn[:, l]}
        xs = trunk_layer(xs, c, lp, ctx)[0]
    y_prompt = rms_norm(xp, final_g)
    y_sample = rms_norm(xs, final_g)
    return (y_prompt, y_sample, jnp.stack(k_list, axis=1), jnp.stack(v_list, axis=1),
            jnp.stack(sr_list, axis=1).astype(x_prompt.dtype), jnp.stack(sg_list, axis=1).astype(x_prompt.dtype))
```

```python
import jax
import jax.numpy as jnp
import numpy as np
from jax import lax
from jax.experimental import pallas as pl
from jax.experimental.pallas import tpu as pltpu

F32 = jnp.float32
BF16 = jnp.bfloat16

D_MODEL = 1024
DEPTH = 2
GRID_W = 64
N_MOD = 6
EPS = 1e-6
GN_EPS = 64e-5
A_HEADS = 8
A_HD = 64
A_W = A_HEADS * A_HD
A_LORA_W = 64
A_LORA_A = 64
A_LORA_G = 128
A_DECAY_SCALE = 0.606531
B_HEADS = 4
B_HD = 128
B_W = B_HEADS * B_HD
B_CONV = 5
C_HEADS = 8
C_KV = 2
C_HD = 64
C_W = C_HEADS * C_HD
C_KVW = C_KV * C_HD
ROPE_BASE = 10000.0
N_BRANCH = 3
IN_SPLITS = (A_W, A_W, A_W, 2 * A_LORA_W, 2 * A_LORA_A, A_LORA_G,
             B_W, B_W, B_W, 2 * B_HEADS, 2 * B_HEADS, B_W,
             C_W, C_KVW, C_KVW, N_BRANCH * D_MODEL)
P_HEADS = 8
P_NKEYS = 128
P_TOPK = 16
P_DKEY = 256
P_DHALF = P_DKEY // 2

V7X_VMEM_LIMIT_BYTES = 48 * 1024 * 1024
LANE = 128
NEG_INF = float('-inf')
HIGHEST = lax.Precision.HIGHEST


def _mm_kernel(x_ref, w_ref, o_ref):
    o_ref[...] = jnp.dot(x_ref[...].astype(BF16), w_ref[...], preferred_element_type=F32)


def _pick_tile(n, candidates):
    for c in candidates:
        if n % c == 0:
            return c
    return n


def pmm(x, w):
    M, K = x.shape
    N = w.shape[1]
    n_pad = (-N) % LANE
    wb = w.astype(BF16)
    if n_pad:
        wb = jnp.pad(wb, ((0, 0), (0, n_pad)))
    Np = N + n_pad
    tm = _pick_tile(M, (1024, 512, 256, 128, 64, 32, 16, 8))
    tn = _pick_tile(Np, (512, 256, 128))
    out = pl.pallas_call(
        _mm_kernel,
        grid=(M // tm, Np // tn),
        in_specs=[pl.BlockSpec((tm, K), lambda i, j: (i, 0)),
                  pl.BlockSpec((K, tn), lambda i, j: (0, j))],
        out_specs=pl.BlockSpec((tm, tn), lambda i, j: (i, j)),
        out_shape=jax.ShapeDtypeStruct((M, Np), F32),
        compiler_params=pltpu.CompilerParams(
            dimension_semantics=("arbitrary", "arbitrary"),
            vmem_limit_bytes=V7X_VMEM_LIMIT_BYTES),
        name="pmm",
    )(x, wb)
    return out[:, :N] if n_pad else out


def pmm3(x, w):
    B, T, K = x.shape
    return pmm(x.reshape(B * T, K), w).reshape(B, T, w.shape[1])


def rms_norm(x, g):
    x32 = x.astype(F32)
    y = x32 * lax.rsqrt(jnp.mean(x32 * x32, axis=-1, keepdims=True) + EPS)
    return (y * g.astype(F32)).astype(x.dtype)


def l2_normalize(x):
    x32 = x.astype(F32)
    return x32 * lax.rsqrt(jnp.sum(x32 * x32, axis=-1, keepdims=True) + EPS)


def split_cols(p, sizes):
    cuts = np.cumsum(np.asarray(sizes))[:-1].tolist()
    return jnp.split(p, cuts, axis=-1)


def axial_rope(x):
    B, T, H, D = x.shape
    n_rows = T // GRID_W
    n_freq = D // 4
    rows = jnp.repeat(jnp.arange(n_rows, dtype=F32), GRID_W)
    cols = jnp.tile(jnp.arange(GRID_W, dtype=F32), n_rows)
    inv = ROPE_BASE ** (-jnp.arange(n_freq, dtype=F32) / n_freq)
    ang = jnp.stack([rows[:, None] * inv, cols[:, None] * inv], axis=1)
    cos = jnp.cos(ang)[None, :, None]
    sin = jnp.sin(ang)[None, :, None]
    xr = x.astype(F32).reshape(B, T, H, 2, 2, n_freq)
    x1 = xr[..., 0, :]
    x2 = xr[..., 1, :]
    out = jnp.stack([x1 * cos - x2 * sin, x2 * cos + x1 * sin], axis=-2)
    return out.reshape(B, T, H, D).astype(x.dtype)


ATTN_TQ = 256


def _attn_kernel(q_ref, kt_ref, v_ref, o_ref):
    n_heads, tq, hd = q_ref.shape[1], q_ref.shape[2], q_ref.shape[3]
    n_kv = kt_ref.shape[1]
    grp = n_heads // n_kv
    scale = hd ** -0.5
    for g in range(n_kv):
        q = q_ref[0, g * grp:(g + 1) * grp].reshape(grp * tq, hd).astype(BF16)
        s = jnp.dot(q, kt_ref[0, g], preferred_element_type=F32) * scale
        p = jnp.exp(s - jnp.max(s, axis=-1, keepdims=True))
        p = (p / jnp.sum(p, axis=-1, keepdims=True)).astype(BF16)
        o = jnp.dot(p, v_ref[0, g], preferred_element_type=F32)
        o_ref[0, g * grp:(g + 1) * grp] = o.reshape(grp, tq, hd)


def blocked_attention(q, k, v):
    B, T, H, D = q.shape
    S, KV = k.shape[1], k.shape[2]
    tq = min(ATTN_TQ, T)
    qh = jnp.transpose(q, (0, 2, 1, 3))
    kt = jnp.transpose(k, (0, 2, 3, 1)).astype(BF16)
    vh = jnp.transpose(v, (0, 2, 1, 3)).astype(BF16)
    o = pl.pallas_call(
        _attn_kernel,
        grid=(B, T // tq),
        in_specs=[pl.BlockSpec((1, H, tq, D), lambda b, i: (b, 0, i, 0)),
                  pl.BlockSpec((1, KV, D, S), lambda b, i: (b, 0, 0, 0)),
                  pl.BlockSpec((1, KV, S, D), lambda b, i: (b, 0, 0, 0))],
        out_specs=pl.BlockSpec((1, H, tq, D), lambda b, i: (b, 0, i, 0)),
        out_shape=jax.ShapeDtypeStruct((B, H, T, D), F32),
        compiler_params=pltpu.CompilerParams(
            dimension_semantics=("arbitrary", "arbitrary"), vmem_limit_bytes=V7X_VMEM_LIMIT_BYTES),
        name="gqa_attention",
    )(qh, kt, vh)
    return jnp.transpose(o, (0, 2, 1, 3))


RWKV_CHUNK = 128


def _mm(a, b):
    return jnp.dot(a.astype(BF16), b.astype(BF16), preferred_element_type=F32)


def _mm_nt(a, b):
    return lax.dot_general(a.astype(BF16), b.astype(BF16), (((1,), (1,)), ((), ())), preferred_element_type=F32)


def _mm_tn(a, b):
    return lax.dot_general(a.astype(BF16), b.astype(BF16), (((0,), (0,)), ((), ())), preferred_element_type=F32)


TRI_BASE = 8


def _unit_tri_inverses(l_mats):
    n = l_mats[0].shape[0]
    row = lax.broadcasted_iota(jnp.int32, (n, n), 0)
    col = lax.broadcasted_iota(jnp.int32, (n, n), 1)

    def same_block(b):
        shift = b.bit_length() - 1
        return jnp.right_shift(row, shift) == jnp.right_shift(col, shift)

    base = same_block(TRI_BASE)
    eye = jnp.where(row == col, 1.0, 0.0)
    ps = [jnp.where(base, l, 0.0) for l in l_mats]
    xs = [eye + p for p in ps]
    span = 2
    while span < TRI_BASE:
        ps = [_mm(p, p) for p in ps]
        xs = [x + _mm(x, p) for x, p in zip(xs, ps)]
        span *= 2
    b = TRI_BASE
    while b < n:
        pair = same_block(2 * b) & jnp.logical_not(same_block(b))
        ts = [_mm(x, jnp.where(pair, l, 0.0)) for x, l in zip(xs, l_mats)]
        xs = [x + _mm(t, x) for x, t in zip(xs, ts)]
        b *= 2
    return xs


def _rwkv_chunk_kernel(incl_ref, strict_ref, ka_ref, r_ref, k_ref, v_ref, kk_ref, lw_ref, a_ref, s0_ref,
                       y_ref, sfin_ref, s_ref):
    c = pl.program_id(2)

    @pl.when(c == 0)
    def _():
        s_ref[...] = s0_ref[0, 0]

    incl = incl_ref[0]
    strict = strict_ref[0]
    n_heads, hd = s_ref.shape[0], s_ref.shape[1]
    heads = range(n_heads)
    r = r_ref[0]
    v = v_ref[0]
    kk = kk_ref[0]
    lw = lw_ref[0, 0]
    a = a_ref[0, 0]
    k_d = k_ref[0] * (1.0 + (a - 1.0) * ka_ref[...])
    bb = kk * a
    lam = jnp.dot(incl, lw, precision=HIGHEST, preferred_element_type=F32)
    lam_end = jnp.sum(lw, axis=0, keepdims=True)
    e_neg = jnp.exp(-lam)
    e_end = jnp.exp(lam_end - lam)
    s_scale = jnp.exp(lam_end)
    ar_all = jnp.concatenate([-kk * jnp.exp(lam - lw), r * jnp.exp(lam)], axis=0).astype(BF16)
    bk_all = jnp.concatenate([bb * e_neg, k_d * e_neg], axis=0).astype(BF16)
    v_all = v.astype(BF16)
    be_all = (bb * e_end).astype(BF16)
    ke_all = (k_d * e_end).astype(BF16)
    cs = r.shape[0]

    def head(x, h):
        return x[:, h * hd:(h + 1) * hd]

    s_old = [s_ref[h] for h in heads]
    ars = [head(ar_all, h) for h in heads]
    vs = [head(v_all, h) for h in heads]
    ps = [_mm_nt(ars[h], s_old[h]) for h in heads]
    ms = [_mm_nt(ars[h], head(bk_all, h)) for h in heads]
    rhs = [ps[h][:cs] + _mm(ms[h][:cs, cs:] * strict, vs[h]) for h in heads]
    invs = _unit_tri_inverses([ms[h][:cs, :cs] * strict for h in heads])
    us = [_mm(invs[h], rhs[h]) for h in heads]
    ys = [ps[h][cs:] + _mm(ms[h][cs:, :cs] * incl, us[h]) + _mm(ms[h][cs:, cs:] * incl, vs[h]) for h in heads]
    y_ref[0, 0] = jnp.concatenate(ys, axis=1)
    for h in heads:
        s_ref[h] = (s_old[h] * head(s_scale, h) + _mm_tn(us[h], head(be_all, h)) + _mm_tn(vs[h], head(ke_all, h)))

    @pl.when(c == pl.num_programs(2) - 1)
    def _():
        sfin_ref[0, 0] = s_ref[...]


def _order_masks(n):
    idx = jnp.arange(n)
    incl = jnp.stack([idx[:, None] >= idx[None, :], idx[:, None] <= idx[None, :]]).astype(F32)
    strict = jnp.stack([idx[:, None] > idx[None, :], idx[:, None] < idx[None, :]]).astype(F32)
    return incl, strict


def rwkv_scan(r, k, v, kk, logw, a, k_a, s0):
    B, T, W = r.shape
    H, N = s0.shape[2], s0.shape[3]
    C = RWKV_CHUNK
    nc = T // C
    incl, strict = _order_masks(C)

    def chunk(d, c):
        return c + d * (nc - 1 - 2 * c)

    seq_spec = pl.BlockSpec((1, C, W), lambda b, d, c: (b, chunk(d, c), 0))
    dir_spec = pl.BlockSpec((1, 1, C, W), lambda b, d, c: (d, b, chunk(d, c), 0))
    state_spec = pl.BlockSpec((1, 1, H, N, N), lambda b, d, c: (b, d, 0, 0, 0))
    return pl.pallas_call(
        _rwkv_chunk_kernel,
        grid=(B, 2, nc),
        in_specs=[pl.BlockSpec((1, C, C), lambda b, d, c: (d, 0, 0)),
                  pl.BlockSpec((1, C, C), lambda b, d, c: (d, 0, 0)),
                  pl.BlockSpec((1, W), lambda b, d, c: (0, 0)),
                  seq_spec, seq_spec, seq_spec, seq_spec, dir_spec, dir_spec, state_spec],
        out_specs=[dir_spec, state_spec],
        out_shape=[jax.ShapeDtypeStruct((2, B, T, W), F32),
                   jax.ShapeDtypeStruct((B, 2, H, N, N), F32)],
        scratch_shapes=[pltpu.VMEM((H, N, N), F32)],
        compiler_params=pltpu.CompilerParams(
            dimension_semantics=("arbitrary", "arbitrary", "arbitrary"), vmem_limit_bytes=V7X_VMEM_LIMIT_BYTES),
        name="rwkv_chunk",
    )(incl, strict, k_a, r, k, v, kk, logw, a, s0)


def rwkv7_branch(pr, pk, pv, pw, pa, pg, lp, s0):
    B, T, _ = pr.shape

    def heads(t):
        return t.astype(F32).reshape(B, T, A_HEADS, A_HD)

    r, k, v = heads(pr), heads(pk), heads(pv)
    k_k = lp['rk_kk'].astype(F32).reshape(A_HEADS, A_HD)
    k_a = lp['rk_ka'].astype(F32).reshape(A_HEADS, A_HD)
    r_k = lp['rk_rk'].astype(F32)
    kk = l2_normalize(k * k_k)
    pw = pw.astype(F32).reshape(B, T, 2, A_LORA_W)
    pa = pa.astype(F32).reshape(B, T, 2, A_LORA_A)
    logw = jnp.stack([-A_DECAY_SCALE * jax.nn.sigmoid(lp['rk_w0'][d] + pmm3(jnp.tanh(pw[:, :, d]), lp['rk_w_up'][d]))
                      for d in range(2)])
    a = jnp.stack([jax.nn.sigmoid(lp['rk_a0'][d] + pmm3(pa[:, :, d], lp['rk_a_up'][d])) for d in range(2)])
    y_dir, s_fin = rwkv_scan(pr.astype(F32), pk.astype(F32), pv.astype(F32), kk.reshape(B, T, A_W), logw, a,
                             k_a.reshape(1, A_W), s0)
    y = heads(y_dir[0] + y_dir[1])
    for d in range(2):
        k_d = k * (1.0 + (heads(a[d]) - 1.0) * k_a)
        y = y + jnp.sum(r * k_d * r_k, axis=-1, keepdims=True) * v
    mu = jnp.mean(y, axis=-1, keepdims=True)
    var = jnp.mean(jnp.square(y - mu), axis=-1, keepdims=True)
    yn = ((y - mu) * lax.rsqrt(var + GN_EPS)).reshape(B, T, A_W) * lp['rk_ln_g'] + lp['rk_ln_b']
    g = pmm3(jax.nn.sigmoid(pg.astype(F32)), lp['rk_g_up'])
    return (yn * g).astype(pr.dtype), s_fin


def short_conv(x, w):
    C = x.shape[-1]
    y = lax.conv_general_dilated(x, w[:, None, :].astype(x.dtype), window_strides=(1,),
                                 padding=[(B_CONV // 2, B_CONV // 2)],
                                 dimension_numbers=('NWC', 'WIO', 'NWC'), feature_group_count=C)
    return jax.nn.silu(y)


GDN_CHUNK = 128


def _gdn_chunk_kernel(incl_ref, strict_ref, q_ref, k_ref, v_ref, g_ref, beta_ref, s0_ref, o_ref, sfin_ref, s_ref):
    c = pl.program_id(2)

    @pl.when(c == 0)
    def _():
        s_ref[...] = s0_ref[0, 0]

    incl = incl_ref[0]
    strict = strict_ref[0]
    cs = incl.shape[0]
    eye = (lax.broadcasted_iota(jnp.int32, (cs, cs), 0) == lax.broadcasted_iota(jnp.int32, (cs, cs), 1)).astype(F32)
    hd = s_ref.shape[1]
    heads = range(s_ref.shape[0])

    def head(ref, h):
        return ref[0, :, h * hd:(h + 1) * hd]

    g_rows = [g_ref[0, 0, 0, h:h + 1, :] for h in heads]
    gc_rows = [lax.dot_general(g, incl, (((1,), (1,)), ((), ())), precision=HIGHEST, preferred_element_type=F32)
               for g in g_rows]
    gc_cols = [jnp.sum(eye * g, axis=1, keepdims=True) for g in gc_rows]
    beta_cols = [jnp.sum(eye * beta_ref[0, 0, 0, h:h + 1, :], axis=1, keepdims=True) for h in heads]
    g_ends = [jnp.sum(g, axis=1, keepdims=True) for g in g_rows]
    decays = [jnp.exp(jnp.where(incl > 0, gc_cols[h] - gc_rows[h], NEG_INF)) for h in heads]
    ks = [head(k_ref, h) for h in heads]
    qs = [head(q_ref, h) for h in heads]
    kbs = [ks[h] * beta_cols[h] for h in heads]
    k_bf = [k.astype(BF16) for k in ks]
    lmats = [strict * _mm_nt(kbs[h], k_bf[h]) * decays[h] for h in heads]
    aqks = [incl * _mm_nt(qs[h], k_bf[h]) * decays[h] for h in heads]
    tinvs = _unit_tri_inverses([-l for l in lmats])
    s_old = [s_ref[h] for h in heads]
    us = [_mm(tinvs[h], head(v_ref, h) * beta_cols[h]) for h in heads]
    wks = [_mm(tinvs[h], kbs[h] * jnp.exp(gc_cols[h])) for h in heads]
    v_new = [us[h] - _mm(wks[h], s_old[h]) for h in heads]
    outs = [_mm(qs[h] * jnp.exp(gc_cols[h]), s_old[h]) + _mm(aqks[h], v_new[h]) for h in heads]
    o_ref[0, 0] = jnp.concatenate(outs, axis=1)
    for h in heads:
        s_ref[h] = s_old[h] * jnp.exp(g_ends[h]) + _mm_tn(ks[h] * jnp.exp(g_ends[h] - gc_cols[h]), v_new[h])

    @pl.when(c == pl.num_programs(2) - 1)
    def _():
        sfin_ref[0, 0] = s_ref[...]


def gdn_scan(q, k, v, g, beta, s0):
    B, T, W = q.shape
    H, D = B_HEADS, B_HD
    C = GDN_CHUNK
    nc = T // C
    incl, strict = _order_masks(C)

    def rows(t):
        return jnp.swapaxes(t.reshape(2, B, nc, C, H), -1, -2)

    def chunk(d, c):
        return c + d * (nc - 1 - 2 * c)

    seq_spec = pl.BlockSpec((1, C, W), lambda b, d, c: (b, chunk(d, c), 0))
    row_spec = pl.BlockSpec((1, 1, 1, H, C), lambda b, d, c: (d, b, chunk(d, c), 0, 0))
    state_spec = pl.BlockSpec((1, 1, H, D, D), lambda b, d, c: (b, d, 0, 0, 0))
    return pl.pallas_call(
        _gdn_chunk_kernel,
        grid=(B, 2, nc),
        in_specs=[pl.BlockSpec((1, C, C), lambda b, d, c: (d, 0, 0)),
                  pl.BlockSpec((1, C, C), lambda b, d, c: (d, 0, 0)),
                  seq_spec, seq_spec, seq_spec, row_spec, row_spec, state_spec],
        out_specs=[pl.BlockSpec((1, 1, C, W), lambda b, d, c: (d, b, chunk(d, c), 0)), state_spec],
        out_shape=[jax.ShapeDtypeStruct((2, B, T, W), F32),
                   jax.ShapeDtypeStruct((B, 2, H, D, D), F32)],
        scratch_shapes=[pltpu.VMEM((H, D, D), F32)],
        compiler_params=pltpu.CompilerParams(
            dimension_semantics=("arbitrary", "arbitrary", "arbitrary"), vmem_limit_bytes=V7X_VMEM_LIMIT_BYTES),
        name="gdn_chunk",
    )(incl, strict, q, k, v, rows(g), rows(beta), s0)


def gdn_branch(pq, pk, pv, p_alpha, p_beta, pz, lp, s0):
    B, T, _ = pq.shape
    qkv = short_conv(jnp.concatenate([pq, pk, pv], axis=-1), lp['gd_conv'])
    q, k, v = jnp.split(qkv, 3, axis=-1)
    q = (l2_normalize(q.reshape(B, T, B_HEADS, B_HD)) * (B_HD ** -0.5)).reshape(B, T, B_W)
    k = l2_normalize(k.reshape(B, T, B_HEADS, B_HD)).reshape(B, T, B_W)
    v = v.astype(F32)
    alpha = p_alpha.astype(F32).reshape(B, T, 2, B_HEADS)
    beta_in = p_beta.astype(F32).reshape(B, T, 2, B_HEADS)
    g = jnp.stack([-jnp.exp(lp['gd_a_log'][d].astype(F32)) * jax.nn.softplus(alpha[:, :, d] + lp['gd_dt_bias'][d])
                   for d in range(2)])
    beta = jnp.stack([jax.nn.sigmoid(beta_in[:, :, d]) for d in range(2)])
    o_dir, s_fin = gdn_scan(q, k, v, g, beta, s0)
    o = (o_dir[0] + o_dir[1]).reshape(B, T, B_HEADS, B_HD)
    z = jax.nn.silu(pz.reshape(B, T, B_HEADS, B_HD))
    y = rms_norm(o, lp['gd_norm_g']) * z
    return y.reshape(B, T, B_W).astype(pq.dtype), s_fin


def attention_branch(pq, pk, pv, lp, ctx_kv):
    B, T, _ = pq.shape
    q = rms_norm(pq.reshape(B, T, C_HEADS, C_HD), lp['at_q_g'])
    k = rms_norm(pk.reshape(B, T, C_KV, C_HD), lp['at_k_g'])
    v = pv.reshape(B, T, C_KV, C_HD)
    if ctx_kv is None:
        o = blocked_attention(q, k, v)
    else:
        keys = jnp.concatenate([ctx_kv[0].astype(k.dtype), axial_rope(k)], axis=1)
        vals = jnp.concatenate([ctx_kv[1].astype(v.dtype), v], axis=1)
        o = blocked_attention(axial_rope(q), keys, vals)
    return o.reshape(B, T, C_W), k, v


PEER_ROUTE_TM = 256
PEER_TM = 512
PEER_I1_PER_STEP = 8
PEER_TE = PEER_I1_PER_STEP * P_NKEYS
PEER_LANES = 128
N_AUX = 4


def _top16_rows(s):
    rows = []
    cur = s
    for _ in range(P_TOPK):
        m = jnp.max(cur, axis=0, keepdims=True)
        rows.append(m)
        cur = jnp.where(cur == m, NEG_INF, cur)
    return rows


def _rows_to_mat(rows):
    n = len(rows)
    tm = rows[0].shape[1]
    rid = lax.broadcasted_iota(jnp.int32, (n, tm), 0)
    mat = jnp.zeros((n, tm), F32)
    for r, row in enumerate(rows):
        mat = jnp.where(rid == r, row, mat)
    return mat


def _peer_route_kernel(x_ref, wq_ref, keys_ref, st_ref, aux_ref, xb_ref):
    xb = x_ref[...].astype(BF16)
    xb_ref[...] = xb
    q = jnp.dot(xb, wq_ref[...], preferred_element_type=F32).astype(BF16)
    for h in range(P_HEADS):
        tops = []
        for p in range(2):
            c = (h * 2 + p) * P_DHALF
            s = lax.dot_general(keys_ref[h * 2 + p], q[:, c:c + P_DHALF], (((1,), (1,)), ((), ())),
                                preferred_element_type=F32)
            st_ref[h * 2 + p] = s
            tops.append(_top16_rows(s))
        t1_rows, t2_rows = tops
        t2 = _rows_to_mat(t2_rows)
        cands = [t1_rows[a] + t2 for a in range(P_TOPK)]
        m0 = t1_rows[0] + t2_rows[0]
        z = jnp.zeros_like(m0)
        tau = m0
        for _ in range(P_TOPK):
            m = cands[0]
            for cnd in cands[1:]:
                m = jnp.maximum(m, cnd)
            m = jnp.max(m, axis=0, keepdims=True)
            z = z + jnp.exp(m - m0)
            tau = m
            cands = [jnp.where(cnd == m, NEG_INF, cnd) for cnd in cands]
        aux_ref[h * N_AUX + 0:h * N_AUX + 1, :] = tau
        aux_ref[h * N_AUX + 1:h * N_AUX + 2, :] = t1_rows[0]
        aux_ref[h * N_AUX + 2:h * N_AUX + 3, :] = t2_rows[0]
        aux_ref[h * N_AUX + 3:h * N_AUX + 4, :] = 1.0 / z


def _peer_mix_kernel(xb_ref, st_ref, aux_ref, u_ref, vt_ref, o_ref, e1_ref, e2_ref, acc_ref):
    e = pl.program_id(1)
    tm = xb_ref.shape[0]

    @pl.when(e == 0)
    def _():
        acc_ref[...] = jnp.zeros_like(acc_ref)
        for h in range(P_HEADS):
            m1 = aux_ref[h * N_AUX + 1:h * N_AUX + 2, :]
            m2 = aux_ref[h * N_AUX + 2:h * N_AUX + 3, :]
            iz = aux_ref[h * N_AUX + 3:h * N_AUX + 4, :]
            e1_ref[h] = jnp.exp(st_ref[2 * h] - m1) * iz
            e2_ref[h] = jnp.exp(st_ref[2 * h + 1] - m2)

    hid = lax.dot_general(u_ref[...], xb_ref[...], (((1,), (1,)), ((), ())),
                          preferred_element_type=F32)
    act = jax.nn.gelu(hid)

    i1_rows = pl.ds(pl.multiple_of(e * PEER_I1_PER_STEP, PEER_I1_PER_STEP), PEER_I1_PER_STEP)
    cols = []
    for c0 in range(0, tm, PEER_LANES):
        lanes = pl.ds(c0, PEER_LANES)
        gs = []
        for r in range(PEER_I1_PER_STEP):
            g = jnp.zeros((P_NKEYS, PEER_LANES), F32)
            for h in range(P_HEADS):
                tau = aux_ref[h * N_AUX:h * N_AUX + 1, lanes]
                s1 = st_ref[2 * h, i1_rows, lanes][r:r + 1, :]
                e1 = e1_ref[h, i1_rows, lanes][r:r + 1, :]
                ssum = s1 + st_ref[2 * h + 1, :, lanes]
                g = jnp.where(ssum >= tau, g + e1 * e2_ref[h, :, lanes], g)
            gs.append(g)
        cols.append(jnp.concatenate(gs, axis=0))
    gate = jnp.concatenate(cols, axis=1)
    w = (act * gate).astype(BF16)
    acc_ref[...] += jnp.dot(vt_ref[...], w, preferred_element_type=F32)

    @pl.when(e == pl.num_programs(1) - 1)
    def _():
        o_ref[...] = acc_ref[...].T


def peer(h, wq, sub_keys, u_tab, v_tab):
    B, T, D = h.shape
    N = B * T
    x = h.reshape(N, D)
    n_hp = P_HEADS * 2
    n_exp = u_tab.shape[0]
    keys_b = sub_keys.reshape(n_hp, P_NKEYS, P_DHALF).astype(BF16)
    st, aux, xb = pl.pallas_call(
        _peer_route_kernel,
        grid=(N // PEER_ROUTE_TM,),
        in_specs=[pl.BlockSpec((PEER_ROUTE_TM, D), lambda i: (i, 0)),
                  pl.BlockSpec((D, P_HEADS * P_DKEY), lambda i: (0, 0)),
                  pl.BlockSpec((n_hp, P_NKEYS, P_DHALF), lambda i: (0, 0, 0))],
        out_specs=[pl.BlockSpec((n_hp, P_NKEYS, PEER_ROUTE_TM), lambda i: (0, 0, i)),
                   pl.BlockSpec((P_HEADS * N_AUX, PEER_ROUTE_TM), lambda i: (0, i)),
                   pl.BlockSpec((PEER_ROUTE_TM, D), lambda i: (i, 0))],
        out_shape=[jax.ShapeDtypeStruct((n_hp, P_NKEYS, N), F32),
                   jax.ShapeDtypeStruct((P_HEADS * N_AUX, N), F32),
                   jax.ShapeDtypeStruct((N, D), BF16)],
        compiler_params=pltpu.CompilerParams(
            dimension_semantics=("arbitrary",), vmem_limit_bytes=V7X_VMEM_LIMIT_BYTES),
        name="peer_route",
    )(x, wq.astype(BF16), keys_b)

    out = pl.pallas_call(
        _peer_mix_kernel,
        grid=(N // PEER_TM, n_exp // PEER_TE),
        in_specs=[pl.BlockSpec((PEER_TM, D), lambda j, e: (j, 0)),
                  pl.BlockSpec((n_hp, P_NKEYS, PEER_TM), lambda j, e: (0, 0, j)),
                  pl.BlockSpec((P_HEADS * N_AUX, PEER_TM), lambda j, e: (0, j)),
                  pl.BlockSpec((PEER_TE, D), lambda j, e: (e, 0)),
                  pl.BlockSpec((D, PEER_TE), lambda j, e: (0, e))],
        out_specs=pl.BlockSpec((PEER_TM, D), lambda j, e: (j, 0)),
        out_shape=jax.ShapeDtypeStruct((N, D), F32),
        scratch_shapes=[pltpu.VMEM((P_HEADS, P_NKEYS, PEER_TM), F32),
                        pltpu.VMEM((P_HEADS, P_NKEYS, PEER_TM), F32),
                        pltpu.VMEM((D, PEER_TM), F32)],
        compiler_params=pltpu.CompilerParams(
            dimension_semantics=("arbitrary", "arbitrary"), vmem_limit_bytes=V7X_VMEM_LIMIT_BYTES),
        name="peer_mix",
    )(xb, st, aux, u_tab.astype(BF16), v_tab.astype(BF16).T)
    return out.reshape(B, T, D)


def trunk_layer(x, cvec, lp, ctx):
    B, T, _ = x.shape
    mod = jax.nn.silu(cvec) @ lp['w_mod'] + lp['b_mod']
    sh1, sc1, g1, sh2, sc2, g2 = [m[:, None, :] for m in jnp.split(mod, N_MOD, axis=-1)]
    h = rms_norm(x, lp['norm1_g']) * (1 + sc1) + sh1
    (ar, ak, av, aw, aa, ag, bq, bk, bv, b_alpha, b_beta, bz, cq, ck, cv, gl) = split_cols(
        pmm3(h, lp['w_in']), IN_SPLITS)
    if ctx is None:
        s_rwkv0 = jnp.zeros((B, 2, A_HEADS, A_HD, A_HD), F32)
        s_gdn0 = jnp.zeros((B, 2, B_HEADS, B_HD, B_HD), F32)
        ctx_kv = None
    else:
        s_rwkv0 = ctx['rwkv'].astype(F32)
        s_gdn0 = ctx['gdn'].astype(F32)
        ctx_kv = (ctx['k'], ctx['v'])
    ya, s_rwkv = rwkv7_branch(ar, ak, av, aw, aa, ag, lp, s_rwkv0)
    yb, s_gdn = gdn_branch(bq, bk, bv, b_alpha, b_beta, bz, lp, s_gdn0)
    yc, k_c, v_c = attention_branch(cq, ck, cv, lp, ctx_kv)
    gate = jax.nn.sigmoid(gl.reshape(B, T, N_BRANCH, D_MODEL))
    merged = (gate[:, :, 0] * pmm3(ya, lp['w_br_a']) + gate[:, :, 1] * pmm3(yb, lp['w_br_b'])
              + gate[:, :, 2] * pmm3(yc, lp['w_br_c']))
    x = x + g1 * pmm3(merged, lp['w_out'])
    h2 = rms_norm(x, lp['norm2_g']) * (1 + sc2) + sh2
    x = x + g2 * peer(h2, lp['pk_wq'], lp['pk_keys'], lp['pk_u'], lp['pk_v'])
    return x, k_c, v_c, s_rwkv, s_gdn


_PARAM_NAMES = ('w_mod', 'b_mod', 'norm1_g', 'w_in',
                'rk_w_up', 'rk_w0', 'rk_a_up', 'rk_a0', 'rk_g_up', 'rk_kk', 'rk_ka', 'rk_rk', 'rk_ln_g', 'rk_ln_b',
                'gd_conv', 'gd_a_log', 'gd_dt_bias', 'gd_norm_g',
                'at_q_g', 'at_k_g',
                'w_br_a', 'w_br_b', 'w_br_c', 'w_out',
                'norm2_g', 'pk_wq', 'pk_keys', 'pk_u', 'pk_v')


def kernel(x_prompt, x_sample, cache_attn_k, cache_attn_v, state_rwkv, state_gdn, c, c_ctx, w_mod, b_mod, norm1_g, w_in, rk_w_up, rk_w0, rk_a_up, rk_a0, rk_g_up, rk_kk, rk_ka, rk_rk, rk_ln_g, rk_ln_b, gd_conv, gd_a_log, gd_dt_bias, gd_norm_g, at_q_g, at_k_g, w_br_a, w_br_b, w_br_c, w_out, norm2_g, pk_wq, pk_keys, pk_u, pk_v, final_g):
    stacked = dict(zip(_PARAM_NAMES, (
        w_mod, b_mod, norm1_g, w_in,
        rk_w_up, rk_w0, rk_a_up, rk_a0, rk_g_up, rk_kk, rk_ka, rk_rk, rk_ln_g, rk_ln_b,
        gd_conv, gd_a_log, gd_dt_bias, gd_norm_g,
        at_q_g, at_k_g,
        w_br_a, w_br_b, w_br_c, w_out,
        norm2_g, pk_wq, pk_keys, pk_u, pk_v)))
    xp = x_prompt
    xs = x_sample
    k_list, v_list, sr_list, sg_list = [], [], [], []
    for l in range(DEPTH):
        lp = {name: arr[l] for name, arr in stacked.items()}
        xp, k_c, v_c, s_r, s_g = trunk_layer(xp, c_ctx[None, :], lp, None)
        k_list.append(k_c)
        v_list.append(v_c)
        sr_list.append(s_r)
        sg_list.append(s_g)
        ctx = {'k': cache_attn_k[:, l], 'v': cache_attn_v[:, l], 'rwkv': state_rwkv[:, l], 'gdn': state_gdn[:, l]}
        xs = trunk_layer(xs, c, lp, ctx)[0]
    y_prompt = rms_norm(xp, final_g)
    y_sample = rms_norm(xs, final_g)
    return (y_prompt, y_sample, jnp.stack(k_list, axis=1), jnp.stack(v_list, axis=1),
            jnp.stack(sr_list, axis=1).astype(x_prompt.dtype), jnp.stack(sg_list, axis=1).astype(x_prompt.dtype))
```

```python
import jax
import jax.numpy as jnp
import numpy as np
from jax import lax
from jax.experimental import pallas as pl
from jax.experimental.pallas import tpu as pltpu

F32 = jnp.float32
BF16 = jnp.bfloat16

D_MODEL = 1024
DEPTH = 2
GRID_W = 64
N_MOD = 6
EPS = 1e-6
GN_EPS = 64e-5
A_HEADS = 8
A_HD = 64
A_W = A_HEADS * A_HD
A_LORA_W = 64
A_LORA_A = 64
A_LORA_G = 128
A_DECAY_SCALE = 0.606531
B_HEADS = 4
B_HD = 128
B_W = B_HEADS * B_HD
B_CONV = 5
C_HEADS = 8
C_KV = 2
C_HD = 64
C_W = C_HEADS * C_HD
C_KVW = C_KV * C_HD
ROPE_BASE = 10000.0
N_BRANCH = 3
IN_SPLITS = (A_W, A_W, A_W, 2 * A_LORA_W, 2 * A_LORA_A, A_LORA_G,
             B_W, B_W, B_W, 2 * B_HEADS, 2 * B_HEADS, B_W,
             C_W, C_KVW, C_KVW, N_BRANCH * D_MODEL)
IN_NAMES = ('ar', 'ak', 'av', 'aw', 'aa', 'ag', 'bq', 'bk', 'bv', 'b_alpha', 'b_beta', 'bz', 'cq', 'ck', 'cv', 'gl')
PROJ_ORDER = ('gl', 'ar', 'ak', 'av', 'bq', 'bk', 'bv', 'bz', 'cq', 'ck', 'cv', 'aw', 'aa', 'ag', 'b_alpha', 'b_beta')
P_HEADS = 8
P_NKEYS = 128
P_TOPK = 16
P_DKEY = 256
P_DHALF = P_DKEY // 2

V7X_VMEM_LIMIT_BYTES = 48 * 1024 * 1024
LANE = 128
NEG_INF = float('-inf')
HIGHEST = lax.Precision.HIGHEST


def _mm_kernel(x_ref, w_ref, o_ref):
    o_ref[...] = jnp.dot(x_ref[...].astype(BF16), w_ref[...], preferred_element_type=F32)


def _pick_tile(n, candidates):
    for c in candidates:
        if n % c == 0:
            return c
    return n


def pmm(x, w):
    M, K = x.shape
    N = w.shape[1]
    n_pad = (-N) % LANE
    wb = w.astype(BF16)
    if n_pad:
        wb = jnp.pad(wb, ((0, 0), (0, n_pad)))
    Np = N + n_pad
    tm = _pick_tile(M, (1024, 512, 256, 128, 64, 32, 16, 8))
    tn = _pick_tile(Np, (512, 256, 128))
    out = pl.pallas_call(
        _mm_kernel,
        grid=(M // tm, Np // tn),
        in_specs=[pl.BlockSpec((tm, K), lambda i, j: (i, 0)),
                  pl.BlockSpec((K, tn), lambda i, j: (0, j))],
        out_specs=pl.BlockSpec((tm, tn), lambda i, j: (i, j)),
        out_shape=jax.ShapeDtypeStruct((M, Np), F32),
        compiler_params=pltpu.CompilerParams(
            dimension_semantics=("arbitrary", "arbitrary"),
            vmem_limit_bytes=V7X_VMEM_LIMIT_BYTES),
        name="pmm",
    )(x, wb)
    return out[:, :N] if n_pad else out


def pmm3(x, w):
    B, T, K = x.shape
    return pmm(x.reshape(B * T, K), w).reshape(B, T, w.shape[1])


def _mod_norm(x, g, sc, sh):
    y = x * lax.rsqrt(jnp.mean(x * x, axis=-1, keepdims=True) + EPS)
    return (y * g) * (1.0 + sc) + sh


def _mod_index(n_mod, rows_per_step, rows_per_batch):
    if n_mod == 1:
        return lambda i: 0
    assert rows_per_batch % rows_per_step == 0
    return lambda i: (i * rows_per_step) // rows_per_batch


def _norm_proj_kernel(x_ref, g_ref, sc_ref, sh_ref, w_ref, o_ref, h_ref):
    @pl.when(pl.program_id(1) == 0)
    def _():
        h_ref[...] = _mod_norm(x_ref[...], g_ref[...], sc_ref[0], sh_ref[0]).astype(BF16)

    o_ref[...] = jnp.dot(h_ref[...], w_ref[...], preferred_element_type=F32)


def norm_proj(x, g, sc, sh, w, rows_per_batch):
    N, D = x.shape
    Np = w.shape[1]
    tm = _pick_tile(N, (1024, 512, 256))
    tn = _pick_tile(Np, (512, 256, 128))
    bidx = _mod_index(sc.shape[0], tm, rows_per_batch)
    mod_spec = pl.BlockSpec((1, 1, D), lambda i, j: (bidx(i), 0, 0))
    return pl.pallas_call(
        _norm_proj_kernel,
        grid=(N // tm, Np // tn),
        in_specs=[pl.BlockSpec((tm, D), lambda i, j: (i, 0)),
                  pl.BlockSpec((1, D), lambda i, j: (0, 0)),
                  mod_spec, mod_spec,
                  pl.BlockSpec((D, tn), lambda i, j: (0, j))],
        out_specs=pl.BlockSpec((tm, tn), lambda i, j: (i, j)),
        out_shape=jax.ShapeDtypeStruct((N, Np), F32),
        scratch_shapes=[pltpu.VMEM((tm, D), BF16)],
        compiler_params=pltpu.CompilerParams(
            dimension_semantics=("arbitrary", "arbitrary"), vmem_limit_bytes=V7X_VMEM_LIMIT_BYTES),
        name="norm_proj",
    )(x, g, sc, sh, w)


MERGE_TM = 256


def _merge_out_kernel(ya_ref, yb_ref, yc_ref, ga_ref, gb_ref, gc_ref, x_ref, g1_ref, wbr_ref, wout_ref, o_ref):
    def branch(y_ref, gate_ref, i):
        return jax.nn.sigmoid(gate_ref[...]) * jnp.dot(y_ref[...].astype(BF16), wbr_ref[i], preferred_element_type=F32)

    merged = branch(ya_ref, ga_ref, 0) + branch(yb_ref, gb_ref, 1) + branch(yc_ref, gc_ref, 2)
    o_ref[...] = x_ref[...] + g1_ref[0] * jnp.dot(merged.astype(BF16), wout_ref[...], preferred_element_type=F32)


def merge_out(ya, yb, yc, proj, gate_col_block, x, g1, w_br, w_out, rows_per_batch):
    N, D = x.shape
    W = ya.shape[1]
    tm = MERGE_TM
    bidx = _mod_index(g1.shape[0], tm, rows_per_batch)
    y_spec = pl.BlockSpec((tm, W), lambda i: (i, 0))

    def gate_spec(k):
        return pl.BlockSpec((tm, D), lambda i: (i, gate_col_block + k))

    return pl.pallas_call(
        _merge_out_kernel,
        grid=(N // tm,),
        in_specs=[y_spec, y_spec, y_spec, gate_spec(0), gate_spec(1), gate_spec(2),
                  pl.BlockSpec((tm, D), lambda i: (i, 0)),
                  pl.BlockSpec((1, 1, D), lambda i: (bidx(i), 0, 0)),
                  pl.BlockSpec((N_BRANCH, W, D), lambda i: (0, 0, 0)),
                  pl.BlockSpec((D, D), lambda i: (0, 0))],
        out_specs=pl.BlockSpec((tm, D), lambda i: (i, 0)),
        out_shape=jax.ShapeDtypeStruct((N, D), F32),
        compiler_params=pltpu.CompilerParams(
            dimension_semantics=("arbitrary",), vmem_limit_bytes=V7X_VMEM_LIMIT_BYTES),
        name="merge_out",
    )(ya, yb, yc, proj, proj, proj, x, g1, w_br, w_out)


def rms_norm(x, g):
    x32 = x.astype(F32)
    y = x32 * lax.rsqrt(jnp.mean(x32 * x32, axis=-1, keepdims=True) + EPS)
    return (y * g.astype(F32)).astype(x.dtype)


def l2_normalize(x):
    x32 = x.astype(F32)
    return x32 * lax.rsqrt(jnp.sum(x32 * x32, axis=-1, keepdims=True) + EPS)


def split_cols(p, sizes):
    cuts = np.cumsum(np.asarray(sizes))[:-1].tolist()
    return jnp.split(p, cuts, axis=-1)


def axial_rope(x):
    B, T, H, D = x.shape
    n_rows = T // GRID_W
    n_freq = D // 4
    rows = jnp.repeat(jnp.arange(n_rows, dtype=F32), GRID_W)
    cols = jnp.tile(jnp.arange(GRID_W, dtype=F32), n_rows)
    inv = ROPE_BASE ** (-jnp.arange(n_freq, dtype=F32) / n_freq)
    ang = jnp.stack([rows[:, None] * inv, cols[:, None] * inv], axis=1)
    cos = jnp.cos(ang)[None, :, None]
    sin = jnp.sin(ang)[None, :, None]
    xr = x.astype(F32).reshape(B, T, H, 2, 2, n_freq)
    x1 = xr[..., 0, :]
    x2 = xr[..., 1, :]
    out = jnp.stack([x1 * cos - x2 * sin, x2 * cos + x1 * sin], axis=-2)
    return out.reshape(B, T, H, D).astype(x.dtype)


ATTN_TQ = 256


def _attn_kernel(q_ref, kt_ref, v_ref, o_ref):
    n_heads, tq, hd = q_ref.shape[1], q_ref.shape[2], q_ref.shape[3]
    n_kv = kt_ref.shape[1]
    grp = n_heads // n_kv
    scale = hd ** -0.5
    for g in range(n_kv):
        q = q_ref[0, g * grp:(g + 1) * grp].reshape(grp * tq, hd).astype(BF16)
        s = jnp.dot(q, kt_ref[0, g], preferred_element_type=F32) * scale
        p = jnp.exp(s - jnp.max(s, axis=-1, keepdims=True))
        p = (p / jnp.sum(p, axis=-1, keepdims=True)).astype(BF16)
        o = jnp.dot(p, v_ref[0, g], preferred_element_type=F32)
        o_ref[0, g * grp:(g + 1) * grp] = o.reshape(grp, tq, hd)


def blocked_attention(q, k, v):
    B, T, H, D = q.shape
    S, KV = k.shape[1], k.shape[2]
    tq = min(ATTN_TQ, T)
    qh = jnp.transpose(q, (0, 2, 1, 3))
    kt = jnp.transpose(k, (0, 2, 3, 1)).astype(BF16)
    vh = jnp.transpose(v, (0, 2, 1, 3)).astype(BF16)
    o = pl.pallas_call(
        _attn_kernel,
        grid=(B, T // tq),
        in_specs=[pl.BlockSpec((1, H, tq, D), lambda b, i: (b, 0, i, 0)),
                  pl.BlockSpec((1, KV, D, S), lambda b, i: (b, 0, 0, 0)),
                  pl.BlockSpec((1, KV, S, D), lambda b, i: (b, 0, 0, 0))],
        out_specs=pl.BlockSpec((1, H, tq, D), lambda b, i: (b, 0, i, 0)),
        out_shape=jax.ShapeDtypeStruct((B, H, T, D), F32),
        compiler_params=pltpu.CompilerParams(
            dimension_semantics=("arbitrary", "arbitrary"), vmem_limit_bytes=V7X_VMEM_LIMIT_BYTES),
        name="gqa_attention",
    )(qh, kt, vh)
    return jnp.transpose(o, (0, 2, 1, 3))


RWKV_CHUNK = 128


def _mm(a, b):
    return jnp.dot(a.astype(BF16), b.astype(BF16), preferred_element_type=F32)


def _mm_nt(a, b):
    return lax.dot_general(a.astype(BF16), b.astype(BF16), (((1,), (1,)), ((), ())), preferred_element_type=F32)


def _mm_tn(a, b):
    return lax.dot_general(a.astype(BF16), b.astype(BF16), (((0,), (0,)), ((), ())), preferred_element_type=F32)


TRI_BASE = 8


def _unit_tri_inverses(l_mats):
    n = l_mats[0].shape[0]
    row = lax.broadcasted_iota(jnp.int32, (n, n), 0)
    col = lax.broadcasted_iota(jnp.int32, (n, n), 1)

    def same_block(b):
        shift = b.bit_length() - 1
        return jnp.right_shift(row, shift) == jnp.right_shift(col, shift)

    base = same_block(TRI_BASE)
    eye = jnp.where(row == col, 1.0, 0.0)
    ps = [jnp.where(base, l, 0.0) for l in l_mats]
    xs = [eye + p for p in ps]
    span = 2
    while span < TRI_BASE:
        ps = [_mm(p, p) for p in ps]
        xs = [x + _mm(x, p) for x, p in zip(xs, ps)]
        span *= 2
    b = TRI_BASE
    while b < n:
        pair = same_block(2 * b) & jnp.logical_not(same_block(b))
        ts = [_mm(x, jnp.where(pair, l, 0.0)) for x, l in zip(xs, l_mats)]
        xs = [x + _mm(t, x) for x, t in zip(xs, ts)]
        b *= 2
    return xs


def _rwkv_chunk_kernel(incl_ref, strict_ref, ka_ref, r_ref, k_ref, v_ref, kk_ref, lw_ref, a_ref, s0_ref,
                       y_ref, sfin_ref, s_ref):
    c = pl.program_id(2)

    @pl.when(c == 0)
    def _():
        s_ref[...] = s0_ref[0, 0]

    incl = incl_ref[0]
    strict = strict_ref[0]
    n_heads, hd = s_ref.shape[0], s_ref.shape[1]
    heads = range(n_heads)
    r = r_ref[0]
    v = v_ref[0]
    kk = kk_ref[0]
    lw = lw_ref[0, 0]
    a = a_ref[0, 0]
    k_d = k_ref[0] * (1.0 + (a - 1.0) * ka_ref[...])
    bb = kk * a
    lam = jnp.dot(incl, lw, precision=HIGHEST, preferred_element_type=F32)
    lam_end = jnp.sum(lw, axis=0, keepdims=True)
    e_neg = jnp.exp(-lam)
    e_end = jnp.exp(lam_end - lam)
    s_scale = jnp.exp(lam_end)
    ar_all = jnp.concatenate([-kk * jnp.exp(lam - lw), r * jnp.exp(lam)], axis=0).astype(BF16)
    bk_all = jnp.concatenate([bb * e_neg, k_d * e_neg], axis=0).astype(BF16)
    v_all = v.astype(BF16)
    be_all = (bb * e_end).astype(BF16)
    ke_all = (k_d * e_end).astype(BF16)
    cs = r.shape[0]

    def head(x, h):
        return x[:, h * hd:(h + 1) * hd]

    s_old = [s_ref[h] for h in heads]
    ars = [head(ar_all, h) for h in heads]
    vs = [head(v_all, h) for h in heads]
    ps = [_mm_nt(ars[h], s_old[h]) for h in heads]
    ms = [_mm_nt(ars[h], head(bk_all, h)) for h in heads]
    rhs = [ps[h][:cs] + _mm(ms[h][:cs, cs:] * strict, vs[h]) for h in heads]
    invs = _unit_tri_inverses([ms[h][:cs, :cs] * strict for h in heads])
    us = [_mm(invs[h], rhs[h]) for h in heads]
    ys = [ps[h][cs:] + _mm(ms[h][cs:, :cs] * incl, us[h]) + _mm(ms[h][cs:, cs:] * incl, vs[h]) for h in heads]
    y_ref[0, 0] = jnp.concatenate(ys, axis=1)
    for h in heads:
        s_ref[h] = (s_old[h] * head(s_scale, h) + _mm_tn(us[h], head(be_all, h)) + _mm_tn(vs[h], head(ke_all, h)))

    @pl.when(c == pl.num_programs(2) - 1)
    def _():
        sfin_ref[0, 0] = s_ref[...]


def _order_masks(n):
    idx = jnp.arange(n)
    incl = jnp.stack([idx[:, None] >= idx[None, :], idx[:, None] <= idx[None, :]]).astype(F32)
    strict = jnp.stack([idx[:, None] > idx[None, :], idx[:, None] < idx[None, :]]).astype(F32)
    return incl, strict


def rwkv_scan(proj, rkv_col_blocks, kk, logw, a, k_a, s0):
    B, T, W = kk.shape
    H, N = s0.shape[2], s0.shape[3]
    C = RWKV_CHUNK
    nc = T // C
    incl, strict = _order_masks(C)

    def chunk(d, c):
        return c + d * (nc - 1 - 2 * c)

    def col_spec(j):
        return pl.BlockSpec((1, C, W), lambda b, d, c: (b, chunk(d, c), j))

    seq_spec = pl.BlockSpec((1, C, W), lambda b, d, c: (b, chunk(d, c), 0))
    dir_spec = pl.BlockSpec((1, 1, C, W), lambda b, d, c: (d, b, chunk(d, c), 0))
    state_spec = pl.BlockSpec((1, 1, H, N, N), lambda b, d, c: (b, d, 0, 0, 0))
    return pl.pallas_call(
        _rwkv_chunk_kernel,
        grid=(B, 2, nc),
        in_specs=[pl.BlockSpec((1, C, C), lambda b, d, c: (d, 0, 0)),
                  pl.BlockSpec((1, C, C), lambda b, d, c: (d, 0, 0)),
                  pl.BlockSpec((1, W), lambda b, d, c: (0, 0)),
                  col_spec(rkv_col_blocks[0]), col_spec(rkv_col_blocks[1]), col_spec(rkv_col_blocks[2]),
                  seq_spec, dir_spec, dir_spec, state_spec],
        out_specs=[dir_spec, state_spec],
        out_shape=[jax.ShapeDtypeStruct((2, B, T, W), F32),
                   jax.ShapeDtypeStruct((B, 2, H, N, N), F32)],
        scratch_shapes=[pltpu.VMEM((H, N, N), F32)],
        compiler_params=pltpu.CompilerParams(
            dimension_semantics=("arbitrary", "arbitrary", "arbitrary"), vmem_limit_bytes=V7X_VMEM_LIMIT_BYTES),
        name="rwkv_chunk",
    )(incl, strict, k_a, proj, proj, proj, kk, logw, a, s0)


def rwkv7_branch(pr, pk, pv, pw, pa, pg, lp, s0, proj, rkv_col_blocks):
    B, T, _ = pr.shape

    def heads(t):
        return t.astype(F32).reshape(B, T, A_HEADS, A_HD)

    r, k, v = heads(pr), heads(pk), heads(pv)
    k_k = lp['rk_kk'].astype(F32).reshape(A_HEADS, A_HD)
    k_a = lp['rk_ka'].astype(F32).reshape(A_HEADS, A_HD)
    r_k = lp['rk_rk'].astype(F32)
    kk = l2_normalize(k * k_k)
    pw = pw.astype(F32).reshape(B, T, 2, A_LORA_W)
    pa = pa.astype(F32).reshape(B, T, 2, A_LORA_A)
    logw = jnp.stack([-A_DECAY_SCALE * jax.nn.sigmoid(lp['rk_w0'][d] + pmm3(jnp.tanh(pw[:, :, d]), lp['rk_w_up'][d]))
                      for d in range(2)])
    a = jnp.stack([jax.nn.sigmoid(lp['rk_a0'][d] + pmm3(pa[:, :, d], lp['rk_a_up'][d])) for d in range(2)])
    y_dir, s_fin = rwkv_scan(proj, rkv_col_blocks, kk.reshape(B, T, A_W), logw, a, k_a.reshape(1, A_W), s0)
    y = heads(y_dir[0] + y_dir[1])
    for d in range(2):
        k_d = k * (1.0 + (heads(a[d]) - 1.0) * k_a)
        y = y + jnp.sum(r * k_d * r_k, axis=-1, keepdims=True) * v
    mu = jnp.mean(y, axis=-1, keepdims=True)
    var = jnp.mean(jnp.square(y - mu), axis=-1, keepdims=True)
    yn = ((y - mu) * lax.rsqrt(var + GN_EPS)).reshape(B, T, A_W) * lp['rk_ln_g'] + lp['rk_ln_b']
    g = pmm3(jax.nn.sigmoid(pg.astype(F32)), lp['rk_g_up'])
    return (yn * g).astype(pr.dtype), s_fin


def short_conv(x, w):
    T = x.shape[1]
    half = B_CONV // 2
    xp = jnp.pad(x, ((0, 0), (half, half), (0, 0)))
    y = xp[:, 0:T, :] * w[0].astype(x.dtype)
    for j in range(1, B_CONV):
        y = y + xp[:, j:j + T, :] * w[j].astype(x.dtype)
    return jax.nn.silu(y)


GDN_CHUNK = 128


def _gdn_chunk_kernel(incl_ref, strict_ref, q_ref, k_ref, v_ref, g_ref, beta_ref, s0_ref, o_ref, sfin_ref, s_ref):
    c = pl.program_id(2)

    @pl.when(c == 0)
    def _():
        s_ref[...] = s0_ref[0, 0]

    incl = incl_ref[0]
    strict = strict_ref[0]
    cs = incl.shape[0]
    eye = (lax.broadcasted_iota(jnp.int32, (cs, cs), 0) == lax.broadcasted_iota(jnp.int32, (cs, cs), 1)).astype(F32)
    hd = s_ref.shape[1]
    heads = range(s_ref.shape[0])

    def head(ref, h):
        return ref[0, :, h * hd:(h + 1) * hd]

    g_rows = [g_ref[0, 0, 0, h:h + 1, :] for h in heads]
    gc_rows = [lax.dot_general(g, incl, (((1,), (1,)), ((), ())), precision=HIGHEST, preferred_element_type=F32)
               for g in g_rows]
    gc_cols = [jnp.sum(eye * g, axis=1, keepdims=True) for g in gc_rows]
    beta_cols = [jnp.sum(eye * beta_ref[0, 0, 0, h:h + 1, :], axis=1, keepdims=True) for h in heads]
    g_ends = [jnp.sum(g, axis=1, keepdims=True) for g in g_rows]
    decays = [jnp.exp(jnp.where(incl > 0, gc_cols[h] - gc_rows[h], NEG_INF)) for h in heads]
    ks = [head(k_ref, h) for h in heads]
    qs = [head(q_ref, h) for h in heads]
    kbs = [ks[h] * beta_cols[h] for h in heads]
    k_bf = [k.astype(BF16) for k in ks]
    lmats = [strict * _mm_nt(kbs[h], k_bf[h]) * decays[h] for h in heads]
    aqks = [incl * _mm_nt(qs[h], k_bf[h]) * decays[h] for h in heads]
    tinvs = _unit_tri_inverses([-l for l in lmats])
    s_old = [s_ref[h] for h in heads]
    us = [_mm(tinvs[h], head(v_ref, h) * beta_cols[h]) for h in heads]
    wks = [_mm(tinvs[h], kbs[h] * jnp.exp(gc_cols[h])) for h in heads]
    v_new = [us[h] - _mm(wks[h], s_old[h]) for h in heads]
    outs = [_mm(qs[h] * jnp.exp(gc_cols[h]), s_old[h]) + _mm(aqks[h], v_new[h]) for h in heads]
    o_ref[0, 0] = jnp.concatenate(outs, axis=1)
    for h in heads:
        s_ref[h] = s_old[h] * jnp.exp(g_ends[h]) + _mm_tn(ks[h] * jnp.exp(g_ends[h] - gc_cols[h]), v_new[h])

    @pl.when(c == pl.num_programs(2) - 1)
    def _():
        sfin_ref[0, 0] = s_ref[...]


def gdn_scan(q, k, v, g, beta, s0):
    B, T, W = q.shape
    H, D = B_HEADS, B_HD
    C = GDN_CHUNK
    nc = T // C
    incl, strict = _order_masks(C)

    def rows(t):
        return jnp.swapaxes(t.reshape(2, B, nc, C, H), -1, -2)

    def chunk(d, c):
        return c + d * (nc - 1 - 2 * c)

    seq_spec = pl.BlockSpec((1, C, W), lambda b, d, c: (b, chunk(d, c), 0))
    row_spec = pl.BlockSpec((1, 1, 1, H, C), lambda b, d, c: (d, b, chunk(d, c), 0, 0))
    state_spec = pl.BlockSpec((1, 1, H, D, D), lambda b, d, c: (b, d, 0, 0, 0))
    return pl.pallas_call(
        _gdn_chunk_kernel,
        grid=(B, 2, nc),
        in_specs=[pl.BlockSpec((1, C, C), lambda b, d, c: (d, 0, 0)),
                  pl.BlockSpec((1, C, C), lambda b, d, c: (d, 0, 0)),
                  seq_spec, seq_spec, seq_spec, row_spec, row_spec, state_spec],
        out_specs=[pl.BlockSpec((1, 1, C, W), lambda b, d, c: (d, b, chunk(d, c), 0)), state_spec],
        out_shape=[jax.ShapeDtypeStruct((2, B, T, W), F32),
                   jax.ShapeDtypeStruct((B, 2, H, D, D), F32)],
        scratch_shapes=[pltpu.VMEM((H, D, D), F32)],
        compiler_params=pltpu.CompilerParams(
            dimension_semantics=("arbitrary", "arbitrary", "arbitrary"), vmem_limit_bytes=V7X_VMEM_LIMIT_BYTES),
        name="gdn_chunk",
    )(incl, strict, q, k, v, rows(g), rows(beta), s0)


def gdn_branch(pq, pk, pv, p_alpha, p_beta, pz, lp, s0):
    B, T, _ = pq.shape
    qkv = short_conv(jnp.concatenate([pq, pk, pv], axis=-1), lp['gd_conv'])
    q, k, v = jnp.split(qkv, 3, axis=-1)
    q = (l2_normalize(q.reshape(B, T, B_HEADS, B_HD)) * (B_HD ** -0.5)).reshape(B, T, B_W)
    k = l2_normalize(k.reshape(B, T, B_HEADS, B_HD)).reshape(B, T, B_W)
    v = v.astype(F32)
    alpha = p_alpha.astype(F32).reshape(B, T, 2, B_HEADS)
    beta_in = p_beta.astype(F32).reshape(B, T, 2, B_HEADS)
    g = jnp.stack([-jnp.exp(lp['gd_a_log'][d].astype(F32)) * jax.nn.softplus(alpha[:, :, d] + lp['gd_dt_bias'][d])
                   for d in range(2)])
    beta = jnp.stack([jax.nn.sigmoid(beta_in[:, :, d]) for d in range(2)])
    o_dir, s_fin = gdn_scan(q, k, v, g, beta, s0)
    o = (o_dir[0] + o_dir[1]).reshape(B, T, B_HEADS, B_HD)
    z = jax.nn.silu(pz.reshape(B, T, B_HEADS, B_HD))
    y = rms_norm(o, lp['gd_norm_g']) * z
    return y.reshape(B, T, B_W).astype(pq.dtype), s_fin


def attention_branch(pq, pk, pv, lp, ctx_kv):
    B, T, _ = pq.shape
    q = rms_norm(pq.reshape(B, T, C_HEADS, C_HD), lp['at_q_g'])
    k = rms_norm(pk.reshape(B, T, C_KV, C_HD), lp['at_k_g'])
    v = pv.reshape(B, T, C_KV, C_HD)
    if ctx_kv is None:
        o = blocked_attention(q, k, v)
    else:
        keys = jnp.concatenate([ctx_kv[0].astype(k.dtype), axial_rope(k)], axis=1)
        vals = jnp.concatenate([ctx_kv[1].astype(v.dtype), v], axis=1)
        o = blocked_attention(axial_rope(q), keys, vals)
    return o.reshape(B, T, C_W), k, v


PEER_ROUTE_TM = 256
PEER_TM = 512
PEER_I1_PER_STEP = 8
PEER_TE = PEER_I1_PER_STEP * P_NKEYS
PEER_LANES = 128
N_AUX = 4


def _top16_rows(s):
    rows = []
    cur = s
    for _ in range(P_TOPK):
        m = jnp.max(cur, axis=0, keepdims=True)
        rows.append(m)
        cur = jnp.where(cur == m, NEG_INF, cur)
    return rows


def _rows_to_mat(rows):
    n = len(rows)
    tm = rows[0].shape[1]
    rid = lax.broadcasted_iota(jnp.int32, (n, tm), 0)
    mat = jnp.zeros((n, tm), F32)
    for r, row in enumerate(rows):
        mat = jnp.where(rid == r, row, mat)
    return mat


def _peer_route_kernel(x_ref, ng_ref, sc_ref, sh_ref, wq_ref, keys_ref, st_ref, aux_ref, xb_ref):
    xb = _mod_norm(x_ref[...], ng_ref[...], sc_ref[0], sh_ref[0]).astype(BF16)
    xb_ref[...] = xb
    q = jnp.dot(xb, wq_ref[...], preferred_element_type=F32).astype(BF16)
    for h in range(P_HEADS):
        tops = []
        for p in range(2):
            c = (h * 2 + p) * P_DHALF
            s = lax.dot_general(keys_ref[h * 2 + p], q[:, c:c + P_DHALF], (((1,), (1,)), ((), ())),
                                preferred_element_type=F32)
            st_ref[h * 2 + p] = s
            tops.append(_top16_rows(s))
        t1_rows, t2_rows = tops
        t2 = _rows_to_mat(t2_rows)
        cands = [t1_rows[a] + t2 for a in range(P_TOPK)]
        m0 = t1_rows[0] + t2_rows[0]
        z = jnp.zeros_like(m0)
        tau = m0
        for _ in range(P_TOPK):
            m = cands[0]
            for cnd in cands[1:]:
                m = jnp.maximum(m, cnd)
            m = jnp.max(m, axis=0, keepdims=True)
            z = z + jnp.exp(m - m0)
            tau = m
            cands = [jnp.where(cnd == m, NEG_INF, cnd) for cnd in cands]
        aux_ref[h * N_AUX + 0:h * N_AUX + 1, :] = tau
        aux_ref[h * N_AUX + 1:h * N_AUX + 2, :] = t1_rows[0]
        aux_ref[h * N_AUX + 2:h * N_AUX + 3, :] = t2_rows[0]
        aux_ref[h * N_AUX + 3:h * N_AUX + 4, :] = 1.0 / z


def _peer_mix_kernel(xb_ref, st_ref, aux_ref, u_ref, vt_ref, xres_ref, g2_ref, o_ref,
                     e1_ref, e2_ref, acc_ref, hid_ref, w_ref):
    e = pl.program_id(1)
    tm = xb_ref.shape[0]

    @pl.when(e == 0)
    def _():
        acc_ref[...] = jnp.zeros_like(acc_ref)
        for h in range(P_HEADS):
            m1 = aux_ref[h * N_AUX + 1:h * N_AUX + 2, :]
            m2 = aux_ref[h * N_AUX + 2:h * N_AUX + 3, :]
            iz = aux_ref[h * N_AUX + 3:h * N_AUX + 4, :]
            e1_ref[h] = jnp.exp(st_ref[2 * h] - m1) * iz
            e2_ref[h] = jnp.exp(st_ref[2 * h + 1] - m2)

    i1_rows = pl.ds(pl.multiple_of(e * PEER_I1_PER_STEP, PEER_I1_PER_STEP), PEER_I1_PER_STEP)
    hid_ref[...] = lax.dot_general(u_ref[...], xb_ref[...], (((1,), (1,)), ((), ())),
                                   preferred_element_type=F32)

    def lane_tile(ci, carry):
        lanes = pl.ds(pl.multiple_of(ci * PEER_LANES, PEER_LANES), PEER_LANES)
        for r in range(PEER_I1_PER_STEP):
            rows = slice(r * P_NKEYS, (r + 1) * P_NKEYS)
            g = jnp.zeros((P_NKEYS, PEER_LANES), F32)
            for h in range(P_HEADS):
                tau = aux_ref[h * N_AUX:h * N_AUX + 1, lanes]
                s1 = st_ref[2 * h, i1_rows, lanes][r:r + 1, :]
                e1 = e1_ref[h, i1_rows, lanes][r:r + 1, :]
                ssum = s1 + st_ref[2 * h + 1, :, lanes]
                g = jnp.where(ssum >= tau, g + e1 * e2_ref[h, :, lanes], g)
            w_ref[rows, lanes] = (jax.nn.gelu(hid_ref[rows, lanes]) * g).astype(BF16)
        return carry

    lax.fori_loop(0, tm // PEER_LANES, lane_tile, 0)
    acc_ref[...] += jnp.dot(vt_ref[...], w_ref[...], preferred_element_type=F32)

    @pl.when(e == pl.num_programs(1) - 1)
    def _():
        o_ref[...] = xres_ref[...] + g2_ref[0] * acc_ref[...].T


def peer(x3, norm_g, sc2, sh2, g2, wq, sub_keys, u_tab, v_tab):
    B, T, D = x3.shape
    N = B * T
    x = x3.reshape(N, D)
    n_hp = P_HEADS * 2
    n_exp = u_tab.shape[0]
    keys_b = sub_keys.reshape(n_hp, P_NKEYS, P_DHALF).astype(BF16)
    route_bidx = _mod_index(sc2.shape[0], PEER_ROUTE_TM, T)
    route_mod_spec = pl.BlockSpec((1, 1, D), lambda i: (route_bidx(i), 0, 0))
    mix_bidx = _mod_index(g2.shape[0], PEER_TM, T)
    st, aux, xb = pl.pallas_call(
        _peer_route_kernel,
        grid=(N // PEER_ROUTE_TM,),
        in_specs=[pl.BlockSpec((PEER_ROUTE_TM, D), lambda i: (i, 0)),
                  pl.BlockSpec((1, D), lambda i: (0, 0)),
                  route_mod_spec, route_mod_spec,
                  pl.BlockSpec((D, P_HEADS * P_DKEY), lambda i: (0, 0)),
                  pl.BlockSpec((n_hp, P_NKEYS, P_DHALF), lambda i: (0, 0, 0))],
        out_specs=[pl.BlockSpec((n_hp, P_NKEYS, PEER_ROUTE_TM), lambda i: (0, 0, i)),
                   pl.BlockSpec((P_HEADS * N_AUX, PEER_ROUTE_TM), lambda i: (0, i)),
                   pl.BlockSpec((PEER_ROUTE_TM, D), lambda i: (i, 0))],
        out_shape=[jax.ShapeDtypeStruct((n_hp, P_NKEYS, N), F32),
                   jax.ShapeDtypeStruct((P_HEADS * N_AUX, N), F32),
                   jax.ShapeDtypeStruct((N, D), BF16)],
        compiler_params=pltpu.CompilerParams(
            dimension_semantics=("arbitrary",), vmem_limit_bytes=V7X_VMEM_LIMIT_BYTES),
        name="peer_route",
    )(x, norm_g.reshape(1, D), sc2, sh2, wq.astype(BF16), keys_b)

    out = pl.pallas_call(
        _peer_mix_kernel,
        grid=(N // PEER_TM, n_exp // PEER_TE),
        in_specs=[pl.BlockSpec((PEER_TM, D), lambda j, e: (j, 0)),
                  pl.BlockSpec((n_hp, P_NKEYS, PEER_TM), lambda j, e: (0, 0, j)),
                  pl.BlockSpec((P_HEADS * N_AUX, PEER_TM), lambda j, e: (0, j)),
                  pl.BlockSpec((PEER_TE, D), lambda j, e: (e, 0)),
                  pl.BlockSpec((D, PEER_TE), lambda j, e: (0, e)),
                  pl.BlockSpec((PEER_TM, D), lambda j, e: (j, 0)),
                  pl.BlockSpec((1, 1, D), lambda j, e: (mix_bidx(j), 0, 0))],
        out_specs=pl.BlockSpec((PEER_TM, D), lambda j, e: (j, 0)),
        out_shape=jax.ShapeDtypeStruct((N, D), F32),
        scratch_shapes=[pltpu.VMEM((P_HEADS, P_NKEYS, PEER_TM), F32),
                        pltpu.VMEM((P_HEADS, P_NKEYS, PEER_TM), F32),
                        pltpu.VMEM((D, PEER_TM), F32),
                        pltpu.VMEM((PEER_TE, PEER_TM), F32),
                        pltpu.VMEM((PEER_TE, PEER_TM), BF16)],
        compiler_params=pltpu.CompilerParams(
            dimension_semantics=("arbitrary", "arbitrary"), vmem_limit_bytes=V7X_VMEM_LIMIT_BYTES),
        name="peer_mix",
    )(xb, st, aux, u_tab.astype(BF16), v_tab.astype(BF16).T, x, g2)
    return out.reshape(B, T, D)


def _proj_layout():
    width = dict(zip(IN_NAMES, IN_SPLITS))
    start = dict(zip(IN_NAMES, np.cumsum((0,) + IN_SPLITS[:-1]).tolist()))
    perm, offs, pos = [], {}, 0
    for name in PROJ_ORDER:
        offs[name] = pos
        perm.extend(range(start[name], start[name] + width[name]))
        pos += width[name]
    return np.asarray(perm, np.int32), offs, pos + (-pos) % LANE


def trunk_layer(x, cvec, lp, ctx):
    B, T, _ = x.shape
    mod = jax.nn.silu(cvec) @ lp['w_mod'] + lp['b_mod']
    sh1, sc1, g1, sh2, sc2, g2 = [m[:, None, :] for m in jnp.split(mod, N_MOD, axis=-1)]
    perm, offs, n_proj = _proj_layout()
    w_in = jnp.pad(lp['w_in'][:, perm].astype(BF16), ((0, 0), (0, n_proj - len(perm))))
    proj = norm_proj(x.reshape(B * T, D_MODEL), lp['norm1_g'].reshape(1, D_MODEL), sc1, sh1, w_in, T)
    proj3 = proj.reshape(B, T, n_proj)
    (ar, ak, av, aw, aa, ag, bq, bk, bv, b_alpha, b_beta, bz, cq, ck, cv, gl) = [
        proj3[:, :, offs[n]:offs[n] + IN_SPLITS[IN_NAMES.index(n)]] for n in IN_NAMES]
    if ctx is None:
        s_rwkv0 = jnp.zeros((B, 2, A_HEADS, A_HD, A_HD), F32)
        s_gdn0 = jnp.zeros((B, 2, B_HEADS, B_HD, B_HD), F32)
        ctx_kv = None
    else:
        s_rwkv0 = ctx['rwkv'].astype(F32)
        s_gdn0 = ctx['gdn'].astype(F32)
        ctx_kv = (ctx['k'], ctx['v'])
    rkv_blocks = tuple(offs[n] // A_W for n in ('ar', 'ak', 'av'))
    ya, s_rwkv = rwkv7_branch(ar, ak, av, aw, aa, ag, lp, s_rwkv0, proj3, rkv_blocks)
    yb, s_gdn = gdn_branch(bq, bk, bv, b_alpha, b_beta, bz, lp, s_gdn0)
    yc, k_c, v_c = attention_branch(cq, ck, cv, lp, ctx_kv)
    w_br = jnp.stack([lp['w_br_a'], lp['w_br_b'], lp['w_br_c']]).astype(BF16)
    x2 = merge_out(ya.reshape(B * T, A_W), yb.reshape(B * T, B_W), yc.reshape(B * T, C_W), proj,
                   offs['gl'] // D_MODEL, x.reshape(B * T, D_MODEL), g1, w_br, lp['w_out'].astype(BF16), T)
    x = peer(x2.reshape(B, T, D_MODEL), lp['norm2_g'], sc2, sh2, g2,
             lp['pk_wq'], lp['pk_keys'], lp['pk_u'], lp['pk_v'])
    return x, k_c, v_c, s_rwkv, s_gdn


_PARAM_NAMES = ('w_mod', 'b_mod', 'norm1_g', 'w_in',
                'rk_w_up', 'rk_w0', 'rk_a_up', 'rk_a0', 'rk_g_up', 'rk_kk', 'rk_ka', 'rk_rk', 'rk_ln_g', 'rk_ln_b',
                'gd_conv', 'gd_a_log', 'gd_dt_bias', 'gd_norm_g',
                'at_q_g', 'at_k_g',
                'w_br_a', 'w_br_b', 'w_br_c', 'w_out',
                'norm2_g', 'pk_wq', 'pk_keys', 'pk_u', 'pk_v')


def kernel(x_prompt, x_sample, cache_attn_k, cache_attn_v, state_rwkv, state_gdn, c, c_ctx, w_mod, b_mod, norm1_g, w_in, rk_w_up, rk_w0, rk_a_up, rk_a0, rk_g_up, rk_kk, rk_ka, rk_rk, rk_ln_g, rk_ln_b, gd_conv, gd_a_log, gd_dt_bias, gd_norm_g, at_q_g, at_k_g, w_br_a, w_br_b, w_br_c, w_out, norm2_g, pk_wq, pk_keys, pk_u, pk_v, final_g):
    stacked = dict(zip(_PARAM_NAMES, (
        w_mod, b_mod, norm1_g, w_in,
        rk_w_up, rk_w0, rk_a_up, rk_a0, rk_g_up, rk_kk, rk_ka, rk_rk, rk_ln_g, rk_ln_b,
        gd_conv, gd_a_log, gd_dt_bias, gd_norm_g,
        at_q_g, at_k_g,
        w_br_a, w_br_b, w_br_c, w_out,
        norm2_g, pk_wq, pk_keys, pk_u, pk_v)))
    xp = x_prompt
    xs = x_sample
    k_list, v_list, sr_list, sg_list = [], [], [], []
    for l in range(DEPTH):
        lp = {name: arr[l] for name, arr in stacked.items()}
        xp, k_c, v_c, s_r, s_g = trunk_layer(xp, c_ctx[None, :], lp, None)
        k_list.append(k_c)
        v_list.append(v_c)
        sr_list.append(s_r)
        sg_list.append(s_g)
        ctx = {'k': cache_attn_k[:, l], 'v': cache_attn_v[:, l], 'rwkv': state_rwkv[:, l], 'gdn': state_gdn[:, l]}
        xs = trunk_layer(xs, c, lp, ctx)[0]
    y_prompt = rms_norm(xp, final_g)
    y_sample = rms_norm(xs, final_g)
    return (y_prompt, y_sample, jnp.stack(k_list, axis=1), jnp.stack(v_list, axis=1),
            jnp.stack(sr_list, axis=1).astype(x_prompt.dtype), jnp.stack(sg_list, axis=1).astype(x_prompt.dtype))
```

```python
import jax
import jax.numpy as jnp
import numpy as np
from jax import lax
from jax.experimental import pallas as pl
from jax.experimental.pallas import tpu as pltpu

F32 = jnp.float32
BF16 = jnp.bfloat16

D_MODEL = 1024
DEPTH = 2
GRID_W = 64
N_MOD = 6
EPS = 1e-6
GN_EPS = 64e-5
A_HEADS = 8
A_HD = 64
A_W = A_HEADS * A_HD
A_LORA_W = 64
A_LORA_A = 64
A_LORA_G = 128
A_DECAY_SCALE = 0.606531
B_HEADS = 4
B_HD = 128
B_W = B_HEADS * B_HD
B_CONV = 5
C_HEADS = 8
C_KV = 2
C_HD = 64
C_W = C_HEADS * C_HD
C_KVW = C_KV * C_HD
ROPE_BASE = 10000.0
N_BRANCH = 3
IN_SPLITS = (A_W, A_W, A_W, 2 * A_LORA_W, 2 * A_LORA_A, A_LORA_G,
             B_W, B_W, B_W, 2 * B_HEADS, 2 * B_HEADS, B_W,
             C_W, C_KVW, C_KVW, N_BRANCH * D_MODEL)
IN_NAMES = ('ar', 'ak', 'av', 'aw', 'aa', 'ag', 'bq', 'bk', 'bv', 'b_alpha', 'b_beta', 'bz', 'cq', 'ck', 'cv', 'gl')
PROJ_ORDER = ('gl', 'ar', 'ak', 'av', 'bq', 'bk', 'bv', 'bz', 'cq', 'ck', 'cv', 'aw', 'aa', 'ag', 'b_alpha', 'b_beta')
P_HEADS = 8
P_NKEYS = 128
P_TOPK = 16
P_DKEY = 256
P_DHALF = P_DKEY // 2

V7X_VMEM_LIMIT_BYTES = 48 * 1024 * 1024
LANE = 128
NEG_INF = float('-inf')
HIGHEST = lax.Precision.HIGHEST


def _mm_kernel(x_ref, w_ref, o_ref):
    o_ref[...] = jnp.dot(x_ref[...].astype(BF16), w_ref[...], preferred_element_type=F32)


def _pick_tile(n, candidates):
    for c in candidates:
        if n % c == 0:
            return c
    return n


def pmm(x, w):
    M, K = x.shape
    N = w.shape[1]
    n_pad = (-N) % LANE
    wb = w.astype(BF16)
    if n_pad:
        wb = jnp.pad(wb, ((0, 0), (0, n_pad)))
    Np = N + n_pad
    tm = _pick_tile(M, (1024, 512, 256, 128, 64, 32, 16, 8))
    tn = _pick_tile(Np, (512, 256, 128))
    out = pl.pallas_call(
        _mm_kernel,
        grid=(M // tm, Np // tn),
        in_specs=[pl.BlockSpec((tm, K), lambda i, j: (i, 0)),
                  pl.BlockSpec((K, tn), lambda i, j: (0, j))],
        out_specs=pl.BlockSpec((tm, tn), lambda i, j: (i, j)),
        out_shape=jax.ShapeDtypeStruct((M, Np), F32),
        compiler_params=pltpu.CompilerParams(
            dimension_semantics=("arbitrary", "arbitrary"),
            vmem_limit_bytes=V7X_VMEM_LIMIT_BYTES),
        name="pmm",
    )(x, wb)
    return out[:, :N] if n_pad else out


def pmm3(x, w):
    B, T, K = x.shape
    return pmm(x.reshape(B * T, K), w).reshape(B, T, w.shape[1])


def _mod_norm(x, g, sc, sh):
    y = x * lax.rsqrt(jnp.mean(x * x, axis=-1, keepdims=True) + EPS)
    return (y * g) * (1.0 + sc) + sh


def _mod_index(n_mod, rows_per_step, rows_per_batch):
    if n_mod == 1:
        return lambda i: 0
    assert rows_per_batch % rows_per_step == 0
    return lambda i: (i * rows_per_step) // rows_per_batch


def _norm_proj_kernel(x_ref, g_ref, sc_ref, sh_ref, w_ref, o_ref, h_ref):
    @pl.when(pl.program_id(1) == 0)
    def _():
        h_ref[...] = _mod_norm(x_ref[...], g_ref[...], sc_ref[0], sh_ref[0]).astype(BF16)

    o_ref[...] = jnp.dot(h_ref[...], w_ref[...], preferred_element_type=F32)


def norm_proj(x, g, sc, sh, w, rows_per_batch):
    N, D = x.shape
    Np = w.shape[1]
    tm = _pick_tile(N, (1024, 512, 256))
    tn = _pick_tile(Np, (512, 256, 128))
    bidx = _mod_index(sc.shape[0], tm, rows_per_batch)
    mod_spec = pl.BlockSpec((1, 1, D), lambda i, j: (bidx(i), 0, 0))
    return pl.pallas_call(
        _norm_proj_kernel,
        grid=(N // tm, Np // tn),
        in_specs=[pl.BlockSpec((tm, D), lambda i, j: (i, 0)),
                  pl.BlockSpec((1, D), lambda i, j: (0, 0)),
                  mod_spec, mod_spec,
                  pl.BlockSpec((D, tn), lambda i, j: (0, j))],
        out_specs=pl.BlockSpec((tm, tn), lambda i, j: (i, j)),
        out_shape=jax.ShapeDtypeStruct((N, Np), F32),
        scratch_shapes=[pltpu.VMEM((tm, D), BF16)],
        compiler_params=pltpu.CompilerParams(
            dimension_semantics=("arbitrary", "arbitrary"), vmem_limit_bytes=V7X_VMEM_LIMIT_BYTES),
        name="norm_proj",
    )(x, g, sc, sh, w)


MERGE_TM = 256


def _merge_out_kernel(ya_ref, yb_ref, yc_ref, ga_ref, gb_ref, gc_ref, x_ref, g1_ref, wbr_ref, wout_ref, o_ref):
    def branch(y_ref, gate_ref, i):
        return jax.nn.sigmoid(gate_ref[...]) * jnp.dot(y_ref[...].astype(BF16), wbr_ref[i], preferred_element_type=F32)

    merged = branch(ya_ref, ga_ref, 0) + branch(yb_ref, gb_ref, 1) + branch(yc_ref, gc_ref, 2)
    o_ref[...] = x_ref[...] + g1_ref[0] * jnp.dot(merged.astype(BF16), wout_ref[...], preferred_element_type=F32)


def merge_out(ya, yb, yc, proj, gate_col_block, x, g1, w_br, w_out, rows_per_batch):
    N, D = x.shape
    W = ya.shape[1]
    tm = MERGE_TM
    bidx = _mod_index(g1.shape[0], tm, rows_per_batch)
    y_spec = pl.BlockSpec((tm, W), lambda i: (i, 0))

    def gate_spec(k):
        return pl.BlockSpec((tm, D), lambda i: (i, gate_col_block + k))

    return pl.pallas_call(
        _merge_out_kernel,
        grid=(N // tm,),
        in_specs=[y_spec, y_spec, y_spec, gate_spec(0), gate_spec(1), gate_spec(2),
                  pl.BlockSpec((tm, D), lambda i: (i, 0)),
                  pl.BlockSpec((1, 1, D), lambda i: (bidx(i), 0, 0)),
                  pl.BlockSpec((N_BRANCH, W, D), lambda i: (0, 0, 0)),
                  pl.BlockSpec((D, D), lambda i: (0, 0))],
        out_specs=pl.BlockSpec((tm, D), lambda i: (i, 0)),
        out_shape=jax.ShapeDtypeStruct((N, D), F32),
        compiler_params=pltpu.CompilerParams(
            dimension_semantics=("arbitrary",), vmem_limit_bytes=V7X_VMEM_LIMIT_BYTES),
        name="merge_out",
    )(ya, yb, yc, proj, proj, proj, x, g1, w_br, w_out)


def rms_norm(x, g):
    x32 = x.astype(F32)
    y = x32 * lax.rsqrt(jnp.mean(x32 * x32, axis=-1, keepdims=True) + EPS)
    return (y * g.astype(F32)).astype(x.dtype)


def l2_normalize(x):
    x32 = x.astype(F32)
    return x32 * lax.rsqrt(jnp.sum(x32 * x32, axis=-1, keepdims=True) + EPS)


def split_cols(p, sizes):
    cuts = np.cumsum(np.asarray(sizes))[:-1].tolist()
    return jnp.split(p, cuts, axis=-1)


def axial_rope(x):
    B, T, H, D = x.shape
    n_rows = T // GRID_W
    n_freq = D // 4
    rows = jnp.repeat(jnp.arange(n_rows, dtype=F32), GRID_W)
    cols = jnp.tile(jnp.arange(GRID_W, dtype=F32), n_rows)
    inv = ROPE_BASE ** (-jnp.arange(n_freq, dtype=F32) / n_freq)
    ang = jnp.stack([rows[:, None] * inv, cols[:, None] * inv], axis=1)
    cos = jnp.cos(ang)[None, :, None]
    sin = jnp.sin(ang)[None, :, None]
    xr = x.astype(F32).reshape(B, T, H, 2, 2, n_freq)
    x1 = xr[..., 0, :]
    x2 = xr[..., 1, :]
    out = jnp.stack([x1 * cos - x2 * sin, x2 * cos + x1 * sin], axis=-2)
    return out.reshape(B, T, H, D).astype(x.dtype)


ATTN_TQ = 256


def _attn_kernel(q_ref, kt_ref, v_ref, o_ref):
    n_heads, tq, hd = q_ref.shape[1], q_ref.shape[2], q_ref.shape[3]
    n_kv = kt_ref.shape[1]
    grp = n_heads // n_kv
    scale = hd ** -0.5
    for g in range(n_kv):
        q = q_ref[0, g * grp:(g + 1) * grp].reshape(grp * tq, hd).astype(BF16)
        s = jnp.dot(q, kt_ref[0, g], preferred_element_type=F32) * scale
        p = jnp.exp(s - jnp.max(s, axis=-1, keepdims=True))
        p = (p / jnp.sum(p, axis=-1, keepdims=True)).astype(BF16)
        o = jnp.dot(p, v_ref[0, g], preferred_element_type=F32)
        o_ref[0, g * grp:(g + 1) * grp] = o.reshape(grp, tq, hd)


def blocked_attention(q, k, v):
    B, T, H, D = q.shape
    S, KV = k.shape[1], k.shape[2]
    tq = min(ATTN_TQ, T)
    qh = jnp.transpose(q, (0, 2, 1, 3))
    kt = jnp.transpose(k, (0, 2, 3, 1)).astype(BF16)
    vh = jnp.transpose(v, (0, 2, 1, 3)).astype(BF16)
    o = pl.pallas_call(
        _attn_kernel,
        grid=(B, T // tq),
        in_specs=[pl.BlockSpec((1, H, tq, D), lambda b, i: (b, 0, i, 0)),
                  pl.BlockSpec((1, KV, D, S), lambda b, i: (b, 0, 0, 0)),
                  pl.BlockSpec((1, KV, S, D), lambda b, i: (b, 0, 0, 0))],
        out_specs=pl.BlockSpec((1, H, tq, D), lambda b, i: (b, 0, i, 0)),
        out_shape=jax.ShapeDtypeStruct((B, H, T, D), F32),
        compiler_params=pltpu.CompilerParams(
            dimension_semantics=("arbitrary", "arbitrary"), vmem_limit_bytes=V7X_VMEM_LIMIT_BYTES),
        name="gqa_attention",
    )(qh, kt, vh)
    return jnp.transpose(o, (0, 2, 1, 3))


RWKV_CHUNK = 128


def _mm(a, b):
    return jnp.dot(a.astype(BF16), b.astype(BF16), preferred_element_type=F32)


def _mm_nt(a, b):
    return lax.dot_general(a.astype(BF16), b.astype(BF16), (((1,), (1,)), ((), ())), preferred_element_type=F32)


def _mm_tn(a, b):
    return lax.dot_general(a.astype(BF16), b.astype(BF16), (((0,), (0,)), ((), ())), preferred_element_type=F32)


TRI_BASE = 8


def _unit_tri_inverses(l_mats):
    n = l_mats[0].shape[0]
    row = lax.broadcasted_iota(jnp.int32, (n, n), 0)
    col = lax.broadcasted_iota(jnp.int32, (n, n), 1)

    def same_block(b):
        shift = b.bit_length() - 1
        return jnp.right_shift(row, shift) == jnp.right_shift(col, shift)

    base = same_block(TRI_BASE)
    eye = jnp.where(row == col, 1.0, 0.0)
    ps = [jnp.where(base, l, 0.0) for l in l_mats]
    xs = [eye + p for p in ps]
    span = 2
    while span < TRI_BASE:
        ps = [_mm(p, p) for p in ps]
        xs = [x + _mm(x, p) for x, p in zip(xs, ps)]
        span *= 2
    b = TRI_BASE
    while b < n:
        pair = same_block(2 * b) & jnp.logical_not(same_block(b))
        ts = [_mm(x, jnp.where(pair, l, 0.0)) for x, l in zip(xs, l_mats)]
        xs = [x + _mm(t, x) for x, t in zip(xs, ts)]
        b *= 2
    return xs


def _head_sum(x, ones_bd):
    hi = x.astype(BF16)
    rest = x - hi.astype(F32)
    mid = rest.astype(BF16)
    lo = (rest - mid.astype(F32)).astype(BF16)
    return (jnp.dot(hi, ones_bd, preferred_element_type=F32)
            + (jnp.dot(mid, ones_bd, preferred_element_type=F32) + jnp.dot(lo, ones_bd, preferred_element_type=F32)))


def _rwkv_gate_a(pa, aup, a0):
    return jax.nn.sigmoid(a0 + jnp.dot(pa.astype(BF16), aup, preferred_element_type=F32))


def _rwkv_chunk_kernel(incl_ref, strict_ref, ones_ref, kk_ref, ka_ref, wup_ref, w0_ref, aup_ref, a0_ref,
                       r_ref, k_ref, v_ref, pw_ref, pa_ref, s0_ref, y_ref, sfin_ref, s_ref):
    c = pl.program_id(2)

    @pl.when(c == 0)
    def _():
        s_ref[...] = s0_ref[0, 0]

    incl = incl_ref[0]
    strict = strict_ref[0]
    n_heads, hd = s_ref.shape[0], s_ref.shape[1]
    heads = range(n_heads)
    r = r_ref[0]
    v = v_ref[0]
    lw = -A_DECAY_SCALE * jax.nn.sigmoid(
        w0_ref[0] + jnp.dot(jnp.tanh(pw_ref[0]).astype(BF16), wup_ref[0], preferred_element_type=F32))
    a = _rwkv_gate_a(pa_ref[0], aup_ref[0], a0_ref[0])
    kq = k_ref[0] * kk_ref[...]
    kk = kq * lax.rsqrt(_head_sum(kq * kq, ones_ref[...]) + EPS)
    k_d = k_ref[0] * (1.0 + (a - 1.0) * ka_ref[...])
    bb = kk * a
    lam = jnp.dot(incl, lw, precision=HIGHEST, preferred_element_type=F32)
    lam_end = jnp.sum(lw, axis=0, keepdims=True)
    e_neg = jnp.exp(-lam)
    e_end = jnp.exp(lam_end - lam)
    s_scale = jnp.exp(lam_end)
    ar_all = jnp.concatenate([-kk * jnp.exp(lam - lw), r * jnp.exp(lam)], axis=0).astype(BF16)
    bk_all = jnp.concatenate([bb * e_neg, k_d * e_neg], axis=0).astype(BF16)
    v_all = v.astype(BF16)
    be_all = (bb * e_end).astype(BF16)
    ke_all = (k_d * e_end).astype(BF16)
    cs = r.shape[0]

    def head(x, h):
        return x[:, h * hd:(h + 1) * hd]

    s_old = [s_ref[h] for h in heads]
    ars = [head(ar_all, h) for h in heads]
    vs = [head(v_all, h) for h in heads]
    ps = [_mm_nt(ars[h], s_old[h]) for h in heads]
    ms = [_mm_nt(ars[h], head(bk_all, h)) for h in heads]
    rhs = [ps[h][:cs] + _mm(ms[h][:cs, cs:] * strict, vs[h]) for h in heads]
    invs = _unit_tri_inverses([ms[h][:cs, :cs] * strict for h in heads])
    us = [_mm(invs[h], rhs[h]) for h in heads]
    ys = [ps[h][cs:] + _mm(ms[h][cs:, :cs] * incl, us[h]) + _mm(ms[h][cs:, cs:] * incl, vs[h]) for h in heads]
    y_ref[0, 0] = jnp.concatenate(ys, axis=1)
    for h in heads:
        s_ref[h] = (s_old[h] * head(s_scale, h) + _mm_tn(us[h], head(be_all, h)) + _mm_tn(vs[h], head(ke_all, h)))

    @pl.when(c == pl.num_programs(2) - 1)
    def _():
        sfin_ref[0, 0] = s_ref[...]


def _order_masks(n):
    idx = jnp.arange(n)
    incl = jnp.stack([idx[:, None] >= idx[None, :], idx[:, None] <= idx[None, :]]).astype(F32)
    strict = jnp.stack([idx[:, None] > idx[None, :], idx[:, None] < idx[None, :]]).astype(F32)
    return incl, strict


def _lora_rows(w_up):
    z = jnp.zeros_like(w_up[0])
    return jnp.stack([jnp.concatenate([w_up[0], z], axis=0), jnp.concatenate([z, w_up[1]], axis=0)]).astype(BF16)


def rwkv_scan(proj, cols, lp, s0):
    B, T, _ = proj.shape
    H, N = s0.shape[2], s0.shape[3]
    W = H * N
    C = RWKV_CHUNK
    nc = T // C
    incl, strict = _order_masks(C)

    def chunk(d, c):
        return c + d * (nc - 1 - 2 * c)

    def col_spec(j, width):
        return pl.BlockSpec((1, C, width), lambda b, d, c: (b, chunk(d, c), j))

    def full(shape):
        return pl.BlockSpec(shape, lambda b, d, c: (0,) * len(shape))

    def per_dir(shape):
        return pl.BlockSpec((1,) + shape, lambda b, d, c: (d,) + (0,) * len(shape))

    lora = 2 * A_LORA_W
    dir_spec = pl.BlockSpec((1, 1, C, W), lambda b, d, c: (d, b, chunk(d, c), 0))
    state_spec = pl.BlockSpec((1, 1, H, N, N), lambda b, d, c: (b, d, 0, 0, 0))
    return pl.pallas_call(
        _rwkv_chunk_kernel,
        grid=(B, 2, nc),
        in_specs=[per_dir((C, C)), per_dir((C, C)), full((W, W)), full((1, W)), full((1, W)),
                  per_dir((lora, W)), per_dir((1, W)), per_dir((lora, W)), per_dir((1, W)),
                  col_spec(cols['ar'], W), col_spec(cols['ak'], W), col_spec(cols['av'], W),
                  col_spec(cols['aw'], lora), col_spec(cols['aa'], lora), state_spec],
        out_specs=[dir_spec, state_spec],
        out_shape=[jax.ShapeDtypeStruct((2, B, T, W), F32),
                   jax.ShapeDtypeStruct((B, 2, H, N, N), F32)],
        scratch_shapes=[pltpu.VMEM((H, N, N), F32)],
        compiler_params=pltpu.CompilerParams(
            dimension_semantics=("arbitrary", "arbitrary", "arbitrary"), vmem_limit_bytes=V7X_VMEM_LIMIT_BYTES),
        name="rwkv_chunk",
    )(incl, strict, _head_ones(H, N), lp['rk_kk'].reshape(1, W), lp['rk_ka'].reshape(1, W),
      _lora_rows(lp['rk_w_up']), lp['rk_w0'].reshape(2, 1, W), _lora_rows(lp['rk_a_up']), lp['rk_a0'].reshape(2, 1, W),
      proj, proj, proj, proj, proj, s0)


def _head_ones(n_heads, hd):
    hid = jnp.arange(n_heads * hd) // hd
    return (hid[:, None] == hid[None, :]).astype(BF16)


RWKV_POST_T = 256


def _rwkv_post_kernel(ones_ref, ka_ref, rk_ref, lng_ref, lnb_ref, aup_ref, a0_ref, gup_ref,
                      y0_ref, y1_ref, r_ref, k_ref, v_ref, pa_ref, pg_ref, o_ref):
    ones_bd = ones_ref[...]
    r = r_ref[0]
    k = k_ref[0]
    v = v_ref[0]
    y = y0_ref[0, 0] + y1_ref[0, 0]
    for d in range(2):
        k_d = k * (1.0 + (_rwkv_gate_a(pa_ref[0], aup_ref[d], a0_ref[d]) - 1.0) * ka_ref[...])
        y = y + _head_sum(r * k_d * rk_ref[...], ones_bd) * v
    inv_n = 1.0 / (ones_bd.shape[0] // A_HEADS)
    yc = y - _head_sum(y, ones_bd) * inv_n
    var = _head_sum(yc * yc, ones_bd) * inv_n
    yn = yc * lax.rsqrt(var + GN_EPS) * lng_ref[...] + lnb_ref[...]
    gate = jnp.dot(jax.nn.sigmoid(pg_ref[0]).astype(BF16), gup_ref[...], preferred_element_type=F32)
    o_ref[0] = yn * gate


def rwkv_post(y_dir, proj, cols, lp):
    _, B, T, W = y_dir.shape
    tp = min(RWKV_POST_T, T)
    lora = 2 * A_LORA_A

    def col_spec(j, width):
        return pl.BlockSpec((1, tp, width), lambda b, i: (b, i, j))

    def full(shape):
        return pl.BlockSpec(shape, lambda b, i: (0,) * len(shape))

    def y_spec(d):
        return pl.BlockSpec((1, 1, tp, W), lambda b, i: (d, b, i, 0))

    return pl.pallas_call(
        _rwkv_post_kernel,
        grid=(B, T // tp),
        in_specs=[full((W, W)), full((1, W)), full((1, W)), full((1, W)), full((1, W)),
                  full((2, lora, W)), full((2, 1, W)), full((A_LORA_G, W)),
                  y_spec(0), y_spec(1),
                  col_spec(cols['ar'], W), col_spec(cols['ak'], W), col_spec(cols['av'], W),
                  col_spec(cols['aa'], lora), col_spec(cols['ag'], A_LORA_G)],
        out_specs=pl.BlockSpec((1, tp, W), lambda b, i: (b, i, 0)),
        out_shape=jax.ShapeDtypeStruct((B, T, W), F32),
        compiler_params=pltpu.CompilerParams(
            dimension_semantics=("arbitrary", "arbitrary"), vmem_limit_bytes=V7X_VMEM_LIMIT_BYTES),
        name="rwkv_post",
    )(_head_ones(A_HEADS, A_HD), lp['rk_ka'].reshape(1, W), lp['rk_rk'].reshape(1, W),
      lp['rk_ln_g'].reshape(1, W), lp['rk_ln_b'].reshape(1, W),
      _lora_rows(lp['rk_a_up']), lp['rk_a0'].reshape(2, 1, W), lp['rk_g_up'].astype(BF16),
      y_dir, y_dir, proj, proj, proj, proj, proj)


def rwkv7_branch(proj, cols, lp, s0):
    y_dir, s_fin = rwkv_scan(proj, cols, lp, s0)
    return rwkv_post(y_dir, proj, cols, lp), s_fin


def short_conv(x, w):
    T = x.shape[1]
    half = B_CONV // 2
    xp = jnp.pad(x, ((0, 0), (half, half), (0, 0)))
    y = xp[:, 0:T, :] * w[0].astype(x.dtype)
    for j in range(1, B_CONV):
        y = y + xp[:, j:j + T, :] * w[j].astype(x.dtype)
    return jax.nn.silu(y)


GDN_CHUNK = 128


def _gdn_chunk_kernel(incl_ref, strict_ref, q_ref, k_ref, v_ref, g_ref, beta_ref, s0_ref, o_ref, sfin_ref, s_ref):
    c = pl.program_id(2)

    @pl.when(c == 0)
    def _():
        s_ref[...] = s0_ref[0, 0]

    incl = incl_ref[0]
    strict = strict_ref[0]
    cs = incl.shape[0]
    eye = (lax.broadcasted_iota(jnp.int32, (cs, cs), 0) == lax.broadcasted_iota(jnp.int32, (cs, cs), 1)).astype(F32)
    hd = s_ref.shape[1]
    heads = range(s_ref.shape[0])

    def head(ref, h):
        return ref[0, :, h * hd:(h + 1) * hd]

    g_rows = [g_ref[0, 0, 0, h:h + 1, :] for h in heads]
    gc_rows = [lax.dot_general(g, incl, (((1,), (1,)), ((), ())), precision=HIGHEST, preferred_element_type=F32)
               for g in g_rows]
    gc_cols = [jnp.sum(eye * g, axis=1, keepdims=True) for g in gc_rows]
    beta_cols = [jnp.sum(eye * beta_ref[0, 0, 0, h:h + 1, :], axis=1, keepdims=True) for h in heads]
    g_ends = [jnp.sum(g, axis=1, keepdims=True) for g in g_rows]
    decays = [jnp.exp(jnp.where(incl > 0, gc_cols[h] - gc_rows[h], NEG_INF)) for h in heads]
    ks = [head(k_ref, h) for h in heads]
    qs = [head(q_ref, h) for h in heads]
    kbs = [ks[h] * beta_cols[h] for h in heads]
    k_bf = [k.astype(BF16) for k in ks]
    lmats = [strict * _mm_nt(kbs[h], k_bf[h]) * decays[h] for h in heads]
    aqks = [incl * _mm_nt(qs[h], k_bf[h]) * decays[h] for h in heads]
    tinvs = _unit_tri_inverses([-l for l in lmats])
    s_old = [s_ref[h] for h in heads]
    us = [_mm(tinvs[h], head(v_ref, h) * beta_cols[h]) for h in heads]
    wks = [_mm(tinvs[h], kbs[h] * jnp.exp(gc_cols[h])) for h in heads]
    v_new = [us[h] - _mm(wks[h], s_old[h]) for h in heads]
    outs = [_mm(qs[h] * jnp.exp(gc_cols[h]), s_old[h]) + _mm(aqks[h], v_new[h]) for h in heads]
    o_ref[0, 0] = jnp.concatenate(outs, axis=1)
    for h in heads:
        s_ref[h] = s_old[h] * jnp.exp(g_ends[h]) + _mm_tn(ks[h] * jnp.exp(g_ends[h] - gc_cols[h]), v_new[h])

    @pl.when(c == pl.num_programs(2) - 1)
    def _():
        sfin_ref[0, 0] = s_ref[...]


def gdn_scan(q, k, v, g, beta, s0):
    B, T, W = q.shape
    H, D = B_HEADS, B_HD
    C = GDN_CHUNK
    nc = T // C
    incl, strict = _order_masks(C)

    def rows(t):
        return jnp.swapaxes(t.reshape(2, B, nc, C, H), -1, -2)

    def chunk(d, c):
        return c + d * (nc - 1 - 2 * c)

    seq_spec = pl.BlockSpec((1, C, W), lambda b, d, c: (b, chunk(d, c), 0))
    row_spec = pl.BlockSpec((1, 1, 1, H, C), lambda b, d, c: (d, b, chunk(d, c), 0, 0))
    state_spec = pl.BlockSpec((1, 1, H, D, D), lambda b, d, c: (b, d, 0, 0, 0))
    return pl.pallas_call(
        _gdn_chunk_kernel,
        grid=(B, 2, nc),
        in_specs=[pl.BlockSpec((1, C, C), lambda b, d, c: (d, 0, 0)),
                  pl.BlockSpec((1, C, C), lambda b, d, c: (d, 0, 0)),
                  seq_spec, seq_spec, seq_spec, row_spec, row_spec, state_spec],
        out_specs=[pl.BlockSpec((1, 1, C, W), lambda b, d, c: (d, b, chunk(d, c), 0)), state_spec],
        out_shape=[jax.ShapeDtypeStruct((2, B, T, W), F32),
                   jax.ShapeDtypeStruct((B, 2, H, D, D), F32)],
        scratch_shapes=[pltpu.VMEM((H, D, D), F32)],
        compiler_params=pltpu.CompilerParams(
            dimension_semantics=("arbitrary", "arbitrary", "arbitrary"), vmem_limit_bytes=V7X_VMEM_LIMIT_BYTES),
        name="gdn_chunk",
    )(incl, strict, q, k, v, rows(g), rows(beta), s0)


def gdn_branch(pq, pk, pv, p_alpha, p_beta, pz, lp, s0):
    B, T, _ = pq.shape
    qkv = short_conv(jnp.concatenate([pq, pk, pv], axis=-1), lp['gd_conv'])
    q, k, v = jnp.split(qkv, 3, axis=-1)
    q = (l2_normalize(q.reshape(B, T, B_HEADS, B_HD)) * (B_HD ** -0.5)).reshape(B, T, B_W)
    k = l2_normalize(k.reshape(B, T, B_HEADS, B_HD)).reshape(B, T, B_W)
    v = v.astype(F32)
    alpha = p_alpha.astype(F32).reshape(B, T, 2, B_HEADS)
    beta_in = p_beta.astype(F32).reshape(B, T, 2, B_HEADS)
    g = jnp.stack([-jnp.exp(lp['gd_a_log'][d].astype(F32)) * jax.nn.softplus(alpha[:, :, d] + lp['gd_dt_bias'][d])
                   for d in range(2)])
    beta = jnp.stack([jax.nn.sigmoid(beta_in[:, :, d]) for d in range(2)])
    o_dir, s_fin = gdn_scan(q, k, v, g, beta, s0)
    o = (o_dir[0] + o_dir[1]).reshape(B, T, B_HEADS, B_HD)
    z = jax.nn.silu(pz.reshape(B, T, B_HEADS, B_HD))
    y = rms_norm(o, lp['gd_norm_g']) * z
    return y.reshape(B, T, B_W).astype(pq.dtype), s_fin


def attention_branch(pq, pk, pv, lp, ctx_kv):
    B, T, _ = pq.shape
    q = rms_norm(pq.reshape(B, T, C_HEADS, C_HD), lp['at_q_g'])
    k = rms_norm(pk.reshape(B, T, C_KV, C_HD), lp['at_k_g'])
    v = pv.reshape(B, T, C_KV, C_HD)
    if ctx_kv is None:
        o = blocked_attention(q, k, v)
    else:
        keys = jnp.concatenate([ctx_kv[0].astype(k.dtype), axial_rope(k)], axis=1)
        vals = jnp.concatenate([ctx_kv[1].astype(v.dtype), v], axis=1)
        o = blocked_attention(axial_rope(q), keys, vals)
    return o.reshape(B, T, C_W), k, v


PEER_ROUTE_TM = 256
PEER_TM = 512
PEER_I1_PER_STEP = 8
PEER_TE = PEER_I1_PER_STEP * P_NKEYS
PEER_LANES = 128
N_AUX = 4


def _top16_rows(s):
    rows = []
    cur = s
    for _ in range(P_TOPK):
        m = jnp.max(cur, axis=0, keepdims=True)
        rows.append(m)
        cur = jnp.where(cur == m, NEG_INF, cur)
    return rows


def _rows_to_mat(rows):
    n = len(rows)
    tm = rows[0].shape[1]
    rid = lax.broadcasted_iota(jnp.int32, (n, tm), 0)
    mat = jnp.zeros((n, tm), F32)
    for r, row in enumerate(rows):
        mat = jnp.where(rid == r, row, mat)
    return mat


SUBLANES = 8


def _peer_route_kernel(x_ref, ng_ref, sc_ref, sh_ref, wq_ref, keys_ref, st_ref, cut_ref, aux_ref, xb_ref):
    xb = _mod_norm(x_ref[...], ng_ref[...], sc_ref[0], sh_ref[0]).astype(BF16)
    xb_ref[...] = xb
    q = jnp.dot(xb, wq_ref[...], preferred_element_type=F32).astype(BF16)
    tm = xb.shape[0]
    rid = lax.broadcasted_iota(jnp.int32, (SUBLANES, tm), 0)
    for h in range(P_HEADS):
        scores, tops = [], []
        for p in range(2):
            c = (h * 2 + p) * P_DHALF
            s = lax.dot_general(keys_ref[h * 2 + p], q[:, c:c + P_DHALF], (((1,), (1,)), ((), ())),
                                preferred_element_type=F32)
            st_ref[h * 2 + p] = s
            scores.append(s)
            tops.append(_top16_rows(s))
        t1_rows, t2_rows = tops
        t2 = _rows_to_mat(t2_rows)
        cands = [t1_rows[0] + t2]
        for a in range(1, SUBLANES):
            cands.append(jnp.where(rid < P_TOPK // (a + 1), t1_rows[a] + t2[:SUBLANES], NEG_INF))
        cands.append(_rows_to_mat(t1_rows[SUBLANES:]) + t2_rows[0])
        m0 = t1_rows[0] + t2_rows[0]
        z = jnp.zeros_like(m0)
        tau = m0
        for _ in range(P_TOPK):
            m = jnp.max(cands[0], axis=0, keepdims=True)
            for cnd in cands[1:]:
                m = jnp.maximum(m, jnp.max(cnd, axis=0, keepdims=True))
            z = z + jnp.exp(m - m0)
            tau = m
            cands = [jnp.where(cnd == m, NEG_INF, cnd) for cnd in cands]
        cut = jnp.full(scores[0].shape, jnp.inf, F32)
        for b in range(P_TOPK):
            cut = jnp.where(scores[0] + t2_rows[b] >= tau, t2_rows[b], cut)
        cut_ref[h] = cut
        aux_ref[h * N_AUX + 0:h * N_AUX + 1, :] = tau
        aux_ref[h * N_AUX + 1:h * N_AUX + 2, :] = t1_rows[0]
        aux_ref[h * N_AUX + 2:h * N_AUX + 3, :] = t2_rows[0]
        aux_ref[h * N_AUX + 3:h * N_AUX + 4, :] = 1.0 / z


def _peer_mix_kernel(xb_ref, st_ref, cut_ref, aux_ref, u_ref, vt_ref, xres_ref, g2_ref, o_ref,
                     e1_ref, e2_ref, acc_ref):
    e = pl.program_id(1)
    tm = xb_ref.shape[0]

    @pl.when(e == 0)
    def _():
        acc_ref[...] = jnp.zeros_like(acc_ref)
        for h in range(P_HEADS):
            m1 = aux_ref[h * N_AUX + 1:h * N_AUX + 2, :]
            m2 = aux_ref[h * N_AUX + 2:h * N_AUX + 3, :]
            iz = aux_ref[h * N_AUX + 3:h * N_AUX + 4, :]
            e1_ref[h] = jnp.exp(st_ref[2 * h] - m1) * iz
            e2_ref[h] = jnp.exp(st_ref[2 * h + 1] - m2)

    i1_rows = pl.ds(pl.multiple_of(e * PEER_I1_PER_STEP, PEER_I1_PER_STEP), PEER_I1_PER_STEP)
    hid = lax.dot_general(u_ref[...], xb_ref[...], (((1,), (1,)), ((), ())),
                          preferred_element_type=F32)
    act = jax.nn.gelu(hid)
    cols = []
    for c0 in range(0, tm, PEER_LANES):
        lanes = pl.ds(c0, PEER_LANES)
        gs = []
        for r in range(PEER_I1_PER_STEP):
            g = jnp.zeros((P_NKEYS, PEER_LANES), F32)
            for h in range(P_HEADS):
                cut = cut_ref[h, i1_rows, lanes][r:r + 1, :]
                e1 = e1_ref[h, i1_rows, lanes][r:r + 1, :]
                g = jnp.where(st_ref[2 * h + 1, :, lanes] >= cut, g + e1 * e2_ref[h, :, lanes], g)
            gs.append(g)
        cols.append(jnp.concatenate(gs, axis=0))
    gate = jnp.concatenate(cols, axis=1)
    w = (act * gate).astype(BF16)
    acc_ref[...] += jnp.dot(vt_ref[...], w, preferred_element_type=F32)

    @pl.when(e == pl.num_programs(1) - 1)
    def _():
        o_ref[...] = xres_ref[...] + g2_ref[0] * acc_ref[...].T


def peer(x3, norm_g, sc2, sh2, g2, wq, sub_keys, u_tab, v_tab):
    B, T, D = x3.shape
    N = B * T
    x = x3.reshape(N, D)
    n_hp = P_HEADS * 2
    n_exp = u_tab.shape[0]
    keys_b = sub_keys.reshape(n_hp, P_NKEYS, P_DHALF).astype(BF16)
    route_bidx = _mod_index(sc2.shape[0], PEER_ROUTE_TM, T)
    route_mod_spec = pl.BlockSpec((1, 1, D), lambda i: (route_bidx(i), 0, 0))
    mix_bidx = _mod_index(g2.shape[0], PEER_TM, T)
    st, cut, aux, xb = pl.pallas_call(
        _peer_route_kernel,
        grid=(N // PEER_ROUTE_TM,),
        in_specs=[pl.BlockSpec((PEER_ROUTE_TM, D), lambda i: (i, 0)),
                  pl.BlockSpec((1, D), lambda i: (0, 0)),
                  route_mod_spec, route_mod_spec,
                  pl.BlockSpec((D, P_HEADS * P_DKEY), lambda i: (0, 0)),
                  pl.BlockSpec((n_hp, P_NKEYS, P_DHALF), lambda i: (0, 0, 0))],
        out_specs=[pl.BlockSpec((n_hp, P_NKEYS, PEER_ROUTE_TM), lambda i: (0, 0, i)),
                   pl.BlockSpec((P_HEADS, P_NKEYS, PEER_ROUTE_TM), lambda i: (0, 0, i)),
                   pl.BlockSpec((P_HEADS * N_AUX, PEER_ROUTE_TM), lambda i: (0, i)),
                   pl.BlockSpec((PEER_ROUTE_TM, D), lambda i: (i, 0))],
        out_shape=[jax.ShapeDtypeStruct((n_hp, P_NKEYS, N), F32),
                   jax.ShapeDtypeStruct((P_HEADS, P_NKEYS, N), F32),
                   jax.ShapeDtypeStruct((P_HEADS * N_AUX, N), F32),
                   jax.ShapeDtypeStruct((N, D), BF16)],
        compiler_params=pltpu.CompilerParams(
            dimension_semantics=("arbitrary",), vmem_limit_bytes=V7X_VMEM_LIMIT_BYTES),
        name="peer_route",
    )(x, norm_g.reshape(1, D), sc2, sh2, wq.astype(BF16), keys_b)

    out = pl.pallas_call(
        _peer_mix_kernel,
        grid=(N // PEER_TM, n_exp // PEER_TE),
        in_specs=[pl.BlockSpec((PEER_TM, D), lambda j, e: (j, 0)),
                  pl.BlockSpec((n_hp, P_NKEYS, PEER_TM), lambda j, e: (0, 0, j)),
                  pl.BlockSpec((P_HEADS, P_NKEYS, PEER_TM), lambda j, e: (0, 0, j)),
                  pl.BlockSpec((P_HEADS * N_AUX, PEER_TM), lambda j, e: (0, j)),
                  pl.BlockSpec((PEER_TE, D), lambda j, e: (e, 0)),
                  pl.BlockSpec((D, PEER_TE), lambda j, e: (0, e)),
                  pl.BlockSpec((PEER_TM, D), lambda j, e: (j, 0)),
                  pl.BlockSpec((1, 1, D), lambda j, e: (mix_bidx(j), 0, 0))],
        out_specs=pl.BlockSpec((PEER_TM, D), lambda j, e: (j, 0)),
        out_shape=jax.ShapeDtypeStruct((N, D), F32),
        scratch_shapes=[pltpu.VMEM((P_HEADS, P_NKEYS, PEER_TM), F32),
                        pltpu.VMEM((P_HEADS, P_NKEYS, PEER_TM), F32),
                        pltpu.VMEM((D, PEER_TM), F32)],
        compiler_params=pltpu.CompilerParams(
            dimension_semantics=("arbitrary", "arbitrary"), vmem_limit_bytes=V7X_VMEM_LIMIT_BYTES),
        name="peer_mix",
    )(xb, st, cut, aux, u_tab.astype(BF16), v_tab.astype(BF16).T, x, g2)
    return out.reshape(B, T, D)


def _proj_layout():
    width = dict(zip(IN_NAMES, IN_SPLITS))
    start = dict(zip(IN_NAMES, np.cumsum((0,) + IN_SPLITS[:-1]).tolist()))
    perm, offs, pos = [], {}, 0
    for name in PROJ_ORDER:
        offs[name] = pos
        perm.extend(range(start[name], start[name] + width[name]))
        pos += width[name]
    return np.asarray(perm, np.int32), offs, pos + (-pos) % LANE


def trunk_layer(x, cvec, lp, ctx):
    B, T, _ = x.shape
    mod = jax.nn.silu(cvec) @ lp['w_mod'] + lp['b_mod']
    sh1, sc1, g1, sh2, sc2, g2 = [m[:, None, :] for m in jnp.split(mod, N_MOD, axis=-1)]
    perm, offs, n_proj = _proj_layout()
    w_in = jnp.pad(lp['w_in'][:, perm].astype(BF16), ((0, 0), (0, n_proj - len(perm))))
    proj = norm_proj(x.reshape(B * T, D_MODEL), lp['norm1_g'].reshape(1, D_MODEL), sc1, sh1, w_in, T)
    proj3 = proj.reshape(B, T, n_proj)
    (ar, ak, av, aw, aa, ag, bq, bk, bv, b_alpha, b_beta, bz, cq, ck, cv, gl) = [
        proj3[:, :, offs[n]:offs[n] + IN_SPLITS[IN_NAMES.index(n)]] for n in IN_NAMES]
    if ctx is None:
        s_rwkv0 = jnp.zeros((B, 2, A_HEADS, A_HD, A_HD), F32)
        s_gdn0 = jnp.zeros((B, 2, B_HEADS, B_HD, B_HD), F32)
        ctx_kv = None
    else:
        s_rwkv0 = ctx['rwkv'].astype(F32)
        s_gdn0 = ctx['gdn'].astype(F32)
        ctx_kv = (ctx['k'], ctx['v'])
    cols = {n: offs[n] // IN_SPLITS[IN_NAMES.index(n)] for n in ('ar', 'ak', 'av', 'aw', 'aa', 'ag')}
    ya, s_rwkv = rwkv7_branch(proj3, cols, lp, s_rwkv0)
    yb, s_gdn = gdn_branch(bq, bk, bv, b_alpha, b_beta, bz, lp, s_gdn0)
    yc, k_c, v_c = attention_branch(cq, ck, cv, lp, ctx_kv)
    w_br = jnp.stack([lp['w_br_a'], lp['w_br_b'], lp['w_br_c']]).astype(BF16)
    x2 = merge_out(ya.reshape(B * T, A_W), yb.reshape(B * T, B_W), yc.reshape(B * T, C_W), proj,
                   offs['gl'] // D_MODEL, x.reshape(B * T, D_MODEL), g1, w_br, lp['w_out'].astype(BF16), T)
    x = peer(x2.reshape(B, T, D_MODEL), lp['norm2_g'], sc2, sh2, g2,
             lp['pk_wq'], lp['pk_keys'], lp['pk_u'], lp['pk_v'])
    return x, k_c, v_c, s_rwkv, s_gdn


_PARAM_NAMES = ('w_mod', 'b_mod', 'norm1_g', 'w_in',
                'rk_w_up', 'rk_w0', 'rk_a_up', 'rk_a0', 'rk_g_up', 'rk_kk', 'rk_ka', 'rk_rk', 'rk_ln_g', 'rk_ln_b',
                'gd_conv', 'gd_a_log', 'gd_dt_bias', 'gd_norm_g',
                'at_q_g', 'at_k_g',
                'w_br_a', 'w_br_b', 'w_br_c', 'w_out',
                'norm2_g', 'pk_wq', 'pk_keys', 'pk_u', 'pk_v')


def kernel(x_prompt, x_sample, cache_attn_k, cache_attn_v, state_rwkv, state_gdn, c, c_ctx, w_mod, b_mod, norm1_g, w_in, rk_w_up, rk_w0, rk_a_up, rk_a0, rk_g_up, rk_kk, rk_ka, rk_rk, rk_ln_g, rk_ln_b, gd_conv, gd_a_log, gd_dt_bias, gd_norm_g, at_q_g, at_k_g, w_br_a, w_br_b, w_br_c, w_out, norm2_g, pk_wq, pk_keys, pk_u, pk_v, final_g):
    stacked = dict(zip(_PARAM_NAMES, (
        w_mod, b_mod, norm1_g, w_in,
        rk_w_up, rk_w0, rk_a_up, rk_a0, rk_g_up, rk_kk, rk_ka, rk_rk, rk_ln_g, rk_ln_b,
        gd_conv, gd_a_log, gd_dt_bias, gd_norm_g,
        at_q_g, at_k_g,
        w_br_a, w_br_b, w_br_c, w_out,
        norm2_g, pk_wq, pk_keys, pk_u, pk_v)))
    xp = x_prompt
    xs = x_sample
    k_list, v_list, sr_list, sg_list = [], [], [], []
    for l in range(DEPTH):
        lp = {name: arr[l] for name, arr in stacked.items()}
        xp, k_c, v_c, s_r, s_g = trunk_layer(xp, c_ctx[None, :], lp, None)
        k_list.append(k_c)
        v_list.append(v_c)
        sr_list.append(s_r)
        sg_list.append(s_g)
        ctx = {'k': cache_attn_k[:, l], 'v': cache_attn_v[:, l], 'rwkv': state_rwkv[:, l], 'gdn': state_gdn[:, l]}
        xs = trunk_layer(xs, c, lp, ctx)[0]
    y_prompt = rms_norm(xp, final_g)
    y_sample = rms_norm(xs, final_g)
    return (y_prompt, y_sample, jnp.stack(k_list, axis=1), jnp.stack(v_list, axis=1),
            jnp.stack(sr_list, axis=1).astype(x_prompt.dtype), jnp.stack(sg_list, axis=1).astype(x_prompt.dtype))
```

```python
import jax
import jax.numpy as jnp
import numpy as np
from jax import lax
from jax.experimental import pallas as pl
from jax.experimental.pallas import tpu as pltpu

F32 = jnp.float32
BF16 = jnp.bfloat16

D_MODEL = 1024
DEPTH = 2
GRID_W = 64
N_MOD = 6
EPS = 1e-6
GN_EPS = 64e-5
A_HEADS = 8
A_HD = 64
A_W = A_HEADS * A_HD
A_LORA_W = 64
A_LORA_A = 64
A_LORA_G = 128
A_DECAY_SCALE = 0.606531
B_HEADS = 4
B_HD = 128
B_W = B_HEADS * B_HD
B_CONV = 5
C_HEADS = 8
C_KV = 2
C_HD = 64
C_W = C_HEADS * C_HD
C_KVW = C_KV * C_HD
ROPE_BASE = 10000.0
N_BRANCH = 3
IN_SPLITS = (A_W, A_W, A_W, 2 * A_LORA_W, 2 * A_LORA_A, A_LORA_G,
             B_W, B_W, B_W, 2 * B_HEADS, 2 * B_HEADS, B_W,
             C_W, C_KVW, C_KVW, N_BRANCH * D_MODEL)
IN_NAMES = ('ar', 'ak', 'av', 'aw', 'aa', 'ag', 'bq', 'bk', 'bv', 'b_alpha', 'b_beta', 'bz', 'cq', 'ck', 'cv', 'gl')
PROJ_ORDER = ('gl', 'ar', 'ak', 'av', 'bq', 'bk', 'bv', 'bz', 'cq', 'ck', 'cv', 'aw', 'aa', 'ag', 'b_alpha', 'b_beta')
P_HEADS = 8
P_NKEYS = 128
P_TOPK = 16
P_DKEY = 256
P_DHALF = P_DKEY // 2

V7X_VMEM_LIMIT_BYTES = 48 * 1024 * 1024
LANE = 128
NEG_INF = float('-inf')
HIGHEST = lax.Precision.HIGHEST


def _mm_kernel(x_ref, w_ref, o_ref):
    o_ref[...] = jnp.dot(x_ref[...].astype(BF16), w_ref[...], preferred_element_type=F32)


def _pick_tile(n, candidates):
    for c in candidates:
        if n % c == 0:
            return c
    return n


def pmm(x, w):
    M, K = x.shape
    N = w.shape[1]
    n_pad = (-N) % LANE
    wb = w.astype(BF16)
    if n_pad:
        wb = jnp.pad(wb, ((0, 0), (0, n_pad)))
    Np = N + n_pad
    tm = _pick_tile(M, (1024, 512, 256, 128, 64, 32, 16, 8))
    tn = _pick_tile(Np, (512, 256, 128))
    out = pl.pallas_call(
        _mm_kernel,
        grid=(M // tm, Np // tn),
        in_specs=[pl.BlockSpec((tm, K), lambda i, j: (i, 0)),
                  pl.BlockSpec((K, tn), lambda i, j: (0, j))],
        out_specs=pl.BlockSpec((tm, tn), lambda i, j: (i, j)),
        out_shape=jax.ShapeDtypeStruct((M, Np), F32),
        compiler_params=pltpu.CompilerParams(
            dimension_semantics=("arbitrary", "arbitrary"),
            vmem_limit_bytes=V7X_VMEM_LIMIT_BYTES),
        name="pmm",
    )(x, wb)
    return out[:, :N] if n_pad else out


def pmm3(x, w):
    B, T, K = x.shape
    return pmm(x.reshape(B * T, K), w).reshape(B, T, w.shape[1])


def _mod_norm(x, g, sc, sh):
    y = x * lax.rsqrt(jnp.mean(x * x, axis=-1, keepdims=True) + EPS)
    return (y * g) * (1.0 + sc) + sh


def _mod_index(n_mod, rows_per_step, rows_per_batch):
    if n_mod == 1:
        return lambda i: 0
    assert rows_per_batch % rows_per_step == 0
    return lambda i: (i * rows_per_step) // rows_per_batch


def _norm_proj_kernel(x_ref, g_ref, sc_ref, sh_ref, w_ref, o_ref, h_ref):
    @pl.when(pl.program_id(1) == 0)
    def _():
        h_ref[...] = _mod_norm(x_ref[...], g_ref[...], sc_ref[0], sh_ref[0]).astype(BF16)

    o_ref[...] = jnp.dot(h_ref[...], w_ref[...], preferred_element_type=F32)


def norm_proj(x, g, sc, sh, w, rows_per_batch):
    N, D = x.shape
    Np = w.shape[1]
    tm = _pick_tile(N, (1024, 512, 256))
    tn = _pick_tile(Np, (512, 256, 128))
    bidx = _mod_index(sc.shape[0], tm, rows_per_batch)
    mod_spec = pl.BlockSpec((1, 1, D), lambda i, j: (bidx(i), 0, 0))
    return pl.pallas_call(
        _norm_proj_kernel,
        grid=(N // tm, Np // tn),
        in_specs=[pl.BlockSpec((tm, D), lambda i, j: (i, 0)),
                  pl.BlockSpec((1, D), lambda i, j: (0, 0)),
                  mod_spec, mod_spec,
                  pl.BlockSpec((D, tn), lambda i, j: (0, j))],
        out_specs=pl.BlockSpec((tm, tn), lambda i, j: (i, j)),
        out_shape=jax.ShapeDtypeStruct((N, Np), F32),
        scratch_shapes=[pltpu.VMEM((tm, D), BF16)],
        compiler_params=pltpu.CompilerParams(
            dimension_semantics=("arbitrary", "arbitrary"), vmem_limit_bytes=V7X_VMEM_LIMIT_BYTES),
        name="norm_proj",
    )(x, g, sc, sh, w)


MERGE_TM = 256


def _merge_out_kernel(ya_ref, yb_ref, yc_ref, ga_ref, gb_ref, gc_ref, x_ref, g1_ref, wbr_ref, wout_ref, o_ref):
    def branch(y_ref, gate_ref, i):
        return jax.nn.sigmoid(gate_ref[...]) * jnp.dot(y_ref[...].astype(BF16), wbr_ref[i], preferred_element_type=F32)

    merged = branch(ya_ref, ga_ref, 0) + branch(yb_ref, gb_ref, 1) + branch(yc_ref, gc_ref, 2)
    o_ref[...] = x_ref[...] + g1_ref[0] * jnp.dot(merged.astype(BF16), wout_ref[...], preferred_element_type=F32)


def merge_out(ya, yb, yc, proj, gate_col_block, x, g1, w_br, w_out, rows_per_batch):
    N, D = x.shape
    W = ya.shape[1]
    tm = MERGE_TM
    bidx = _mod_index(g1.shape[0], tm, rows_per_batch)
    y_spec = pl.BlockSpec((tm, W), lambda i: (i, 0))

    def gate_spec(k):
        return pl.BlockSpec((tm, D), lambda i: (i, gate_col_block + k))

    return pl.pallas_call(
        _merge_out_kernel,
        grid=(N // tm,),
        in_specs=[y_spec, y_spec, y_spec, gate_spec(0), gate_spec(1), gate_spec(2),
                  pl.BlockSpec((tm, D), lambda i: (i, 0)),
                  pl.BlockSpec((1, 1, D), lambda i: (bidx(i), 0, 0)),
                  pl.BlockSpec((N_BRANCH, W, D), lambda i: (0, 0, 0)),
                  pl.BlockSpec((D, D), lambda i: (0, 0))],
        out_specs=pl.BlockSpec((tm, D), lambda i: (i, 0)),
        out_shape=jax.ShapeDtypeStruct((N, D), F32),
        compiler_params=pltpu.CompilerParams(
            dimension_semantics=("arbitrary",), vmem_limit_bytes=V7X_VMEM_LIMIT_BYTES),
        name="merge_out",
    )(ya, yb, yc, proj, proj, proj, x, g1, w_br, w_out)


def rms_norm(x, g):
    x32 = x.astype(F32)
    y = x32 * lax.rsqrt(jnp.mean(x32 * x32, axis=-1, keepdims=True) + EPS)
    return (y * g.astype(F32)).astype(x.dtype)


def l2_normalize(x):
    x32 = x.astype(F32)
    return x32 * lax.rsqrt(jnp.sum(x32 * x32, axis=-1, keepdims=True) + EPS)


def split_cols(p, sizes):
    cuts = np.cumsum(np.asarray(sizes))[:-1].tolist()
    return jnp.split(p, cuts, axis=-1)


def axial_rope(x):
    B, T, H, D = x.shape
    n_rows = T // GRID_W
    n_freq = D // 4
    rows = jnp.repeat(jnp.arange(n_rows, dtype=F32), GRID_W)
    cols = jnp.tile(jnp.arange(GRID_W, dtype=F32), n_rows)
    inv = ROPE_BASE ** (-jnp.arange(n_freq, dtype=F32) / n_freq)
    ang = jnp.stack([rows[:, None] * inv, cols[:, None] * inv], axis=1)
    cos = jnp.cos(ang)[None, :, None]
    sin = jnp.sin(ang)[None, :, None]
    xr = x.astype(F32).reshape(B, T, H, 2, 2, n_freq)
    x1 = xr[..., 0, :]
    x2 = xr[..., 1, :]
    out = jnp.stack([x1 * cos - x2 * sin, x2 * cos + x1 * sin], axis=-2)
    return out.reshape(B, T, H, D).astype(x.dtype)


ATTN_TQ = 256


def _attn_kernel(q_ref, kt_ref, v_ref, o_ref):
    n_heads, tq, hd = q_ref.shape[1], q_ref.shape[2], q_ref.shape[3]
    n_kv = kt_ref.shape[1]
    grp = n_heads // n_kv
    scale = hd ** -0.5
    for g in range(n_kv):
        q = (q_ref[0, g * grp:(g + 1) * grp].reshape(grp * tq, hd).astype(BF16) * scale).astype(BF16)
        s = jnp.dot(q, kt_ref[0, g], preferred_element_type=F32)
        p = jnp.exp(s - jnp.max(s, axis=-1, keepdims=True))
        p = (p * (1.0 / jnp.sum(p, axis=-1, keepdims=True))).astype(BF16)
        o = jnp.dot(p, v_ref[0, g], preferred_element_type=F32)
        o_ref[0, g * grp:(g + 1) * grp] = o.reshape(grp, tq, hd)


def blocked_attention(q, k, v):
    B, T, H, D = q.shape
    S, KV = k.shape[1], k.shape[2]
    assert D == 4 ** (D.bit_length() // 2), "the kernel folds D ** -0.5 into bf16 q: it must be a power of two"
    tq = min(ATTN_TQ, T)
    qh = jnp.transpose(q, (0, 2, 1, 3))
    kt = jnp.transpose(k, (0, 2, 3, 1)).astype(BF16)
    vh = jnp.transpose(v, (0, 2, 1, 3)).astype(BF16)
    o = pl.pallas_call(
        _attn_kernel,
        grid=(B, T // tq),
        in_specs=[pl.BlockSpec((1, H, tq, D), lambda b, i: (b, 0, i, 0)),
                  pl.BlockSpec((1, KV, D, S), lambda b, i: (b, 0, 0, 0)),
                  pl.BlockSpec((1, KV, S, D), lambda b, i: (b, 0, 0, 0))],
        out_specs=pl.BlockSpec((1, H, tq, D), lambda b, i: (b, 0, i, 0)),
        out_shape=jax.ShapeDtypeStruct((B, H, T, D), F32),
        compiler_params=pltpu.CompilerParams(
            dimension_semantics=("arbitrary", "arbitrary"), vmem_limit_bytes=V7X_VMEM_LIMIT_BYTES),
        name="gqa_attention",
    )(qh, kt, vh)
    return jnp.transpose(o, (0, 2, 1, 3))


RWKV_CHUNK = 128


def _mm(a, b):
    return jnp.dot(a.astype(BF16), b.astype(BF16), preferred_element_type=F32)


def _mm_nt(a, b):
    return lax.dot_general(a.astype(BF16), b.astype(BF16), (((1,), (1,)), ((), ())), preferred_element_type=F32)


def _mm_tn(a, b):
    return lax.dot_general(a.astype(BF16), b.astype(BF16), (((0,), (0,)), ((), ())), preferred_element_type=F32)


TRI_BASE = 8


def _unit_tri_inverses(l_mats):
    n = l_mats[0].shape[0]
    row = lax.broadcasted_iota(jnp.int32, (n, n), 0)
    col = lax.broadcasted_iota(jnp.int32, (n, n), 1)

    def same_block(b):
        shift = b.bit_length() - 1
        return jnp.right_shift(row, shift) == jnp.right_shift(col, shift)

    base = same_block(TRI_BASE)
    eye = jnp.where(row == col, 1.0, 0.0)
    ps = [jnp.where(base, l, 0.0) for l in l_mats]
    xs = [eye + p for p in ps]
    span = 2
    while span < TRI_BASE:
        ps = [_mm(p, p) for p in ps]
        xs = [x + _mm(x, p) for x, p in zip(xs, ps)]
        span *= 2
    b = TRI_BASE
    while b < n:
        pair = same_block(2 * b) & jnp.logical_not(same_block(b))
        ts = [_mm(x, jnp.where(pair, l, 0.0)) for x, l in zip(xs, l_mats)]
        xs = [x + _mm(t, x) for x, t in zip(xs, ts)]
        b *= 2
    return xs


def _head_sum(x, ones_bd):
    hi = x.astype(BF16)
    rest = x - hi.astype(F32)
    mid = rest.astype(BF16)
    lo = (rest - mid.astype(F32)).astype(BF16)
    return (jnp.dot(hi, ones_bd, preferred_element_type=F32)
            + (jnp.dot(mid, ones_bd, preferred_element_type=F32) + jnp.dot(lo, ones_bd, preferred_element_type=F32)))


def _rwkv_gate_a(pa, aup, a0):
    return jax.nn.sigmoid(a0 + jnp.dot(pa.astype(BF16), aup, preferred_element_type=F32))


def _rwkv_chunk_kernel(incl_ref, strict_ref, ones_ref, kk_ref, ka_ref, wup_ref, w0_ref, aup_ref, a0_ref,
                       r_ref, k_ref, v_ref, pw_ref, pa_ref, s0_ref, y_ref, sfin_ref, s_ref):
    c = pl.program_id(2)

    @pl.when(c == 0)
    def _():
        s_ref[...] = s0_ref[0, 0]

    incl = incl_ref[0]
    strict = strict_ref[0]
    n_heads, hd = s_ref.shape[0], s_ref.shape[1]
    heads = range(n_heads)
    r = r_ref[0]
    v = v_ref[0]
    lw = -A_DECAY_SCALE * jax.nn.sigmoid(
        w0_ref[0] + jnp.dot(jnp.tanh(pw_ref[0]).astype(BF16), wup_ref[0], preferred_element_type=F32))
    a = _rwkv_gate_a(pa_ref[0], aup_ref[0], a0_ref[0])
    kq = k_ref[0] * kk_ref[...]
    kk = kq * lax.rsqrt(_head_sum(kq * kq, ones_ref[...]) + EPS)
    k_d = k_ref[0] * (1.0 + (a - 1.0) * ka_ref[...])
    bb = kk * a
    lam = jnp.dot(incl, lw, precision=HIGHEST, preferred_element_type=F32)
    lam_end = jnp.sum(lw, axis=0, keepdims=True)
    e_neg = jnp.exp(-lam)
    e_end = jnp.exp(lam_end - lam)
    s_scale = jnp.exp(lam_end)
    ar_all = jnp.concatenate([-kk * jnp.exp(lam - lw), r * jnp.exp(lam)], axis=0).astype(BF16)
    bk_all = jnp.concatenate([bb * e_neg, k_d * e_neg], axis=0).astype(BF16)
    v_all = v.astype(BF16)
    be_all = (bb * e_end).astype(BF16)
    ke_all = (k_d * e_end).astype(BF16)
    cs = r.shape[0]

    def head(x, h):
        return x[:, h * hd:(h + 1) * hd]

    s_old = [s_ref[h] for h in heads]
    ars = [head(ar_all, h) for h in heads]
    vs = [head(v_all, h) for h in heads]
    ps = [_mm_nt(ars[h], s_old[h]) for h in heads]
    ms = [_mm_nt(ars[h], head(bk_all, h)) for h in heads]
    rhs = [ps[h][:cs] + _mm(ms[h][:cs, cs:] * strict, vs[h]) for h in heads]
    invs = _unit_tri_inverses([ms[h][:cs, :cs] * strict for h in heads])
    us = [_mm(invs[h], rhs[h]) for h in heads]
    ys = [ps[h][cs:] + _mm(ms[h][cs:, :cs] * incl, us[h]) + _mm(ms[h][cs:, cs:] * incl, vs[h]) for h in heads]
    y_ref[0, 0] = jnp.concatenate(ys, axis=1)
    for h in heads:
        s_ref[h] = (s_old[h] * head(s_scale, h) + _mm_tn(us[h], head(be_all, h)) + _mm_tn(vs[h], head(ke_all, h)))

    @pl.when(c == pl.num_programs(2) - 1)
    def _():
        sfin_ref[0, 0] = s_ref[...]


def _order_masks(n):
    idx = jnp.arange(n)
    incl = jnp.stack([idx[:, None] >= idx[None, :], idx[:, None] <= idx[None, :]]).astype(F32)
    strict = jnp.stack([idx[:, None] > idx[None, :], idx[:, None] < idx[None, :]]).astype(F32)
    return incl, strict


def _lora_rows(w_up):
    z = jnp.zeros_like(w_up[0])
    return jnp.stack([jnp.concatenate([w_up[0], z], axis=0), jnp.concatenate([z, w_up[1]], axis=0)]).astype(BF16)


def rwkv_scan(proj, cols, lp, s0):
    B, T, _ = proj.shape
    H, N = s0.shape[2], s0.shape[3]
    W = H * N
    C = RWKV_CHUNK
    nc = T // C
    incl, strict = _order_masks(C)

    def chunk(d, c):
        return c + d * (nc - 1 - 2 * c)

    def col_spec(j, width):
        return pl.BlockSpec((1, C, width), lambda b, d, c: (b, chunk(d, c), j))

    def full(shape):
        return pl.BlockSpec(shape, lambda b, d, c: (0,) * len(shape))

    def per_dir(shape):
        return pl.BlockSpec((1,) + shape, lambda b, d, c: (d,) + (0,) * len(shape))

    lora = 2 * A_LORA_W
    dir_spec = pl.BlockSpec((1, 1, C, W), lambda b, d, c: (d, b, chunk(d, c), 0))
    state_spec = pl.BlockSpec((1, 1, H, N, N), lambda b, d, c: (b, d, 0, 0, 0))
    return pl.pallas_call(
        _rwkv_chunk_kernel,
        grid=(B, 2, nc),
        in_specs=[per_dir((C, C)), per_dir((C, C)), full((W, W)), full((1, W)), full((1, W)),
                  per_dir((lora, W)), per_dir((1, W)), per_dir((lora, W)), per_dir((1, W)),
                  col_spec(cols['ar'], W), col_spec(cols['ak'], W), col_spec(cols['av'], W),
                  col_spec(cols['aw'], lora), col_spec(cols['aa'], lora), state_spec],
        out_specs=[dir_spec, state_spec],
        out_shape=[jax.ShapeDtypeStruct((2, B, T, W), F32),
                   jax.ShapeDtypeStruct((B, 2, H, N, N), F32)],
        scratch_shapes=[pltpu.VMEM((H, N, N), F32)],
        compiler_params=pltpu.CompilerParams(
            dimension_semantics=("arbitrary", "arbitrary", "arbitrary"), vmem_limit_bytes=V7X_VMEM_LIMIT_BYTES),
        name="rwkv_chunk",
    )(incl, strict, _head_ones(H, N), lp['rk_kk'].reshape(1, W), lp['rk_ka'].reshape(1, W),
      _lora_rows(lp['rk_w_up']), lp['rk_w0'].reshape(2, 1, W), _lora_rows(lp['rk_a_up']), lp['rk_a0'].reshape(2, 1, W),
      proj, proj, proj, proj, proj, s0)


def _head_ones(n_heads, hd):
    hid = jnp.arange(n_heads * hd) // hd
    return (hid[:, None] == hid[None, :]).astype(BF16)


RWKV_POST_T = 256


def _rwkv_post_kernel(ones_ref, ka_ref, rk_ref, lng_ref, lnb_ref, aup_ref, a0_ref, gup_ref,
                      y0_ref, y1_ref, r_ref, k_ref, v_ref, pa_ref, pg_ref, o_ref):
    ones_bd = ones_ref[...]
    r = r_ref[0]
    k = k_ref[0]
    v = v_ref[0]
    y = y0_ref[0, 0] + y1_ref[0, 0]
    for d in range(2):
        k_d = k * (1.0 + (_rwkv_gate_a(pa_ref[0], aup_ref[d], a0_ref[d]) - 1.0) * ka_ref[...])
        y = y + _head_sum(r * k_d * rk_ref[...], ones_bd) * v
    inv_n = 1.0 / (ones_bd.shape[0] // A_HEADS)
    yc = y - _head_sum(y, ones_bd) * inv_n
    var = _head_sum(yc * yc, ones_bd) * inv_n
    yn = yc * lax.rsqrt(var + GN_EPS) * lng_ref[...] + lnb_ref[...]
    gate = jnp.dot(jax.nn.sigmoid(pg_ref[0]).astype(BF16), gup_ref[...], preferred_element_type=F32)
    o_ref[0] = yn * gate


def rwkv_post(y_dir, proj, cols, lp):
    _, B, T, W = y_dir.shape
    tp = min(RWKV_POST_T, T)
    lora = 2 * A_LORA_A

    def col_spec(j, width):
        return pl.BlockSpec((1, tp, width), lambda b, i: (b, i, j))

    def full(shape):
        return pl.BlockSpec(shape, lambda b, i: (0,) * len(shape))

    def y_spec(d):
        return pl.BlockSpec((1, 1, tp, W), lambda b, i: (d, b, i, 0))

    return pl.pallas_call(
        _rwkv_post_kernel,
        grid=(B, T // tp),
        in_specs=[full((W, W)), full((1, W)), full((1, W)), full((1, W)), full((1, W)),
                  full((2, lora, W)), full((2, 1, W)), full((A_LORA_G, W)),
                  y_spec(0), y_spec(1),
                  col_spec(cols['ar'], W), col_spec(cols['ak'], W), col_spec(cols['av'], W),
                  col_spec(cols['aa'], lora), col_spec(cols['ag'], A_LORA_G)],
        out_specs=pl.BlockSpec((1, tp, W), lambda b, i: (b, i, 0)),
        out_shape=jax.ShapeDtypeStruct((B, T, W), F32),
        compiler_params=pltpu.CompilerParams(
            dimension_semantics=("arbitrary", "arbitrary"), vmem_limit_bytes=V7X_VMEM_LIMIT_BYTES),
        name="rwkv_post",
    )(_head_ones(A_HEADS, A_HD), lp['rk_ka'].reshape(1, W), lp['rk_rk'].reshape(1, W),
      lp['rk_ln_g'].reshape(1, W), lp['rk_ln_b'].reshape(1, W),
      _lora_rows(lp['rk_a_up']), lp['rk_a0'].reshape(2, 1, W), lp['rk_g_up'].astype(BF16),
      y_dir, y_dir, proj, proj, proj, proj, proj)


def rwkv7_branch(proj, cols, lp, s0):
    y_dir, s_fin = rwkv_scan(proj, cols, lp, s0)
    return rwkv_post(y_dir, proj, cols, lp), s_fin


def short_conv(x, w):
    T = x.shape[1]
    half = B_CONV // 2
    xp = jnp.pad(x, ((0, 0), (half, half), (0, 0)))
    y = xp[:, 0:T, :] * w[0].astype(x.dtype)
    for j in range(1, B_CONV):
        y = y + xp[:, j:j + T, :] * w[j].astype(x.dtype)
    return jax.nn.silu(y)


GDN_CHUNK = 128


def _gdn_chunk_kernel(incl_ref, strict_ref, q_ref, k_ref, v_ref, g_ref, beta_ref, s0_ref, o_ref, sfin_ref, s_ref):
    c = pl.program_id(2)

    @pl.when(c == 0)
    def _():
        s_ref[...] = s0_ref[0, 0]

    incl = incl_ref[0]
    strict = strict_ref[0]
    cs = incl.shape[0]
    eye = (lax.broadcasted_iota(jnp.int32, (cs, cs), 0) == lax.broadcasted_iota(jnp.int32, (cs, cs), 1)).astype(F32)
    hd = s_ref.shape[1]
    heads = range(s_ref.shape[0])

    def head(ref, h):
        return ref[0, :, h * hd:(h + 1) * hd]

    g_rows = [g_ref[0, 0, 0, h:h + 1, :] for h in heads]
    gc_rows = [lax.dot_general(g, incl, (((1,), (1,)), ((), ())), precision=HIGHEST, preferred_element_type=F32)
               for g in g_rows]
    gc_cols = [jnp.sum(eye * g, axis=1, keepdims=True) for g in gc_rows]
    beta_cols = [jnp.sum(eye * beta_ref[0, 0, 0, h:h + 1, :], axis=1, keepdims=True) for h in heads]
    g_ends = [jnp.sum(g, axis=1, keepdims=True) for g in g_rows]
    decays = [jnp.exp(jnp.where(incl > 0, gc_cols[h] - gc_rows[h], NEG_INF)) for h in heads]
    ks = [head(k_ref, h) for h in heads]
    qs = [head(q_ref, h) for h in heads]
    kbs = [ks[h] * beta_cols[h] for h in heads]
    k_bf = [k.astype(BF16) for k in ks]
    lmats = [strict * _mm_nt(kbs[h], k_bf[h]) * decays[h] for h in heads]
    aqks = [incl * _mm_nt(qs[h], k_bf[h]) * decays[h] for h in heads]
    tinvs = _unit_tri_inverses([-l for l in lmats])
    s_old = [s_ref[h] for h in heads]
    us = [_mm(tinvs[h], head(v_ref, h) * beta_cols[h]) for h in heads]
    wks = [_mm(tinvs[h], kbs[h] * jnp.exp(gc_cols[h])) for h in heads]
    v_new = [us[h] - _mm(wks[h], s_old[h]) for h in heads]
    outs = [_mm(qs[h] * jnp.exp(gc_cols[h]), s_old[h]) + _mm(aqks[h], v_new[h]) for h in heads]
    o_ref[0, 0] = jnp.concatenate(outs, axis=1)
    for h in heads:
        s_ref[h] = s_old[h] * jnp.exp(g_ends[h]) + _mm_tn(ks[h] * jnp.exp(g_ends[h] - gc_cols[h]), v_new[h])

    @pl.when(c == pl.num_programs(2) - 1)
    def _():
        sfin_ref[0, 0] = s_ref[...]


def gdn_scan(q, k, v, g, beta, s0):
    B, T, W = q.shape
    H, D = B_HEADS, B_HD
    C = GDN_CHUNK
    nc = T // C
    incl, strict = _order_masks(C)

    def rows(t):
        return jnp.swapaxes(t.reshape(2, B, nc, C, H), -1, -2)

    def chunk(d, c):
        return c + d * (nc - 1 - 2 * c)

    seq_spec = pl.BlockSpec((1, C, W), lambda b, d, c: (b, chunk(d, c), 0))
    row_spec = pl.BlockSpec((1, 1, 1, H, C), lambda b, d, c: (d, b, chunk(d, c), 0, 0))
    state_spec = pl.BlockSpec((1, 1, H, D, D), lambda b, d, c: (b, d, 0, 0, 0))
    return pl.pallas_call(
        _gdn_chunk_kernel,
        grid=(B, 2, nc),
        in_specs=[pl.BlockSpec((1, C, C), lambda b, d, c: (d, 0, 0)),
                  pl.BlockSpec((1, C, C), lambda b, d, c: (d, 0, 0)),
                  seq_spec, seq_spec, seq_spec, row_spec, row_spec, state_spec],
        out_specs=[pl.BlockSpec((1, 1, C, W), lambda b, d, c: (d, b, chunk(d, c), 0)), state_spec],
        out_shape=[jax.ShapeDtypeStruct((2, B, T, W), F32),
                   jax.ShapeDtypeStruct((B, 2, H, D, D), F32)],
        scratch_shapes=[pltpu.VMEM((H, D, D), F32)],
        compiler_params=pltpu.CompilerParams(
            dimension_semantics=("arbitrary", "arbitrary", "arbitrary"), vmem_limit_bytes=V7X_VMEM_LIMIT_BYTES),
        name="gdn_chunk",
    )(incl, strict, q, k, v, rows(g), rows(beta), s0)


def gdn_branch(pq, pk, pv, p_alpha, p_beta, pz, lp, s0):
    B, T, _ = pq.shape
    qkv = short_conv(jnp.concatenate([pq, pk, pv], axis=-1), lp['gd_conv'])
    q, k, v = jnp.split(qkv, 3, axis=-1)
    q = (l2_normalize(q.reshape(B, T, B_HEADS, B_HD)) * (B_HD ** -0.5)).reshape(B, T, B_W)
    k = l2_normalize(k.reshape(B, T, B_HEADS, B_HD)).reshape(B, T, B_W)
    v = v.astype(F32)
    alpha = p_alpha.astype(F32).reshape(B, T, 2, B_HEADS)
    beta_in = p_beta.astype(F32).reshape(B, T, 2, B_HEADS)
    g = jnp.stack([-jnp.exp(lp['gd_a_log'][d].astype(F32)) * jax.nn.softplus(alpha[:, :, d] + lp['gd_dt_bias'][d])
                   for d in range(2)])
    beta = jnp.stack([jax.nn.sigmoid(beta_in[:, :, d]) for d in range(2)])
    o_dir, s_fin = gdn_scan(q, k, v, g, beta, s0)
    o = (o_dir[0] + o_dir[1]).reshape(B, T, B_HEADS, B_HD)
    z = jax.nn.silu(pz.reshape(B, T, B_HEADS, B_HD))
    y = rms_norm(o, lp['gd_norm_g']) * z
    return y.reshape(B, T, B_W).astype(pq.dtype), s_fin


def attention_branch(pq, pk, pv, lp, ctx_kv):
    B, T, _ = pq.shape
    q = rms_norm(pq.reshape(B, T, C_HEADS, C_HD), lp['at_q_g'])
    k = rms_norm(pk.reshape(B, T, C_KV, C_HD), lp['at_k_g'])
    v = pv.reshape(B, T, C_KV, C_HD)
    if ctx_kv is None:
        o = blocked_attention(q, k, v)
    else:
        keys = jnp.concatenate([ctx_kv[0].astype(k.dtype), axial_rope(k)], axis=1)
        vals = jnp.concatenate([ctx_kv[1].astype(v.dtype), v], axis=1)
        o = blocked_attention(axial_rope(q), keys, vals)
    return o.reshape(B, T, C_W), k, v


PEER_ROUTE_TM = 256
PEER_TM = 512
PEER_I1_PER_STEP = 8
PEER_TE = PEER_I1_PER_STEP * P_NKEYS
PEER_LANES = 128


def _top16_rows(s, with_rank=False):
    rows = []
    cur = s
    rank = jnp.full(s.shape, float(P_TOPK), F32)
    for r in range(P_TOPK):
        m = jnp.max(cur, axis=0, keepdims=True)
        rows.append(m)
        hit = cur == m
        if with_rank:
            rank = jnp.where(hit, float(r), rank)
        cur = jnp.where(hit, NEG_INF, cur)
    return (rows, rank) if with_rank else rows


def _rows_to_mat(rows):
    n = len(rows)
    tm = rows[0].shape[1]
    rid = lax.broadcasted_iota(jnp.int32, (n, tm), 0)
    mat = jnp.zeros((n, tm), F32)
    for r, row in enumerate(rows):
        mat = jnp.where(rid == r, row, mat)
    return mat


SUBLANES = 8


def _peer_route_kernel(x_ref, ng_ref, sc_ref, sh_ref, wq_ref, keys_ref, e1_ref, bmax_ref, rank2_ref, e2_ref, xb_ref):
    xb = _mod_norm(x_ref[...], ng_ref[...], sc_ref[0], sh_ref[0]).astype(BF16)
    xb_ref[...] = xb
    q = jnp.dot(xb, wq_ref[...], preferred_element_type=F32).astype(BF16)
    tm = xb.shape[0]
    rid = lax.broadcasted_iota(jnp.int32, (SUBLANES, tm), 0)
    for h in range(P_HEADS):
        s1, s2 = [lax.dot_general(keys_ref[h * 2 + p], q[:, (h * 2 + p) * P_DHALF:(h * 2 + p + 1) * P_DHALF],
                                  (((1,), (1,)), ((), ())), preferred_element_type=F32) for p in range(2)]
        t1_rows = _top16_rows(s1)
        t2_rows, rank2 = _top16_rows(s2, with_rank=True)
        t2 = _rows_to_mat(t2_rows)
        cands = [t1_rows[0] + t2]
        for a in range(1, SUBLANES):
            cands.append(jnp.where(rid < P_TOPK // (a + 1), t1_rows[a] + t2[:SUBLANES], NEG_INF))
        cands.append(_rows_to_mat(t1_rows[SUBLANES:]) + t2_rows[0])
        m0 = t1_rows[0] + t2_rows[0]
        z = jnp.zeros_like(m0)
        tau = m0
        for _ in range(P_TOPK):
            m = jnp.max(cands[0], axis=0, keepdims=True)
            for cnd in cands[1:]:
                m = jnp.maximum(m, jnp.max(cnd, axis=0, keepdims=True))
            z = z + jnp.exp(m - m0)
            tau = m
            cands = [jnp.where(cnd == m, NEG_INF, cnd) for cnd in cands]
        bmax = jnp.full(s1.shape, -1.0, F32)
        for b in range(P_TOPK):
            bmax = jnp.where(s1 + t2_rows[b] >= tau, float(b), bmax)
        e1_ref[h] = jnp.exp(s1 - t1_rows[0]) * (1.0 / z)
        bmax_ref[h] = bmax
        rank2_ref[h] = rank2.astype(BF16)
        e2_ref[h] = jnp.exp(s2 - t2_rows[0]).astype(BF16)


def _peer_mix_kernel(xb_ref, e1_ref, bmax_ref, rank2_ref, e2_ref, u_ref, vt_ref, xres_ref, g2_ref, o_ref, acc_ref):
    e = pl.program_id(1)
    tm = xb_ref.shape[0]

    @pl.when(e == 0)
    def _():
        acc_ref[...] = jnp.zeros_like(acc_ref)

    i1_rows = pl.ds(pl.multiple_of(e * PEER_I1_PER_STEP, PEER_I1_PER_STEP), PEER_I1_PER_STEP)
    hid = lax.dot_general(u_ref[...], xb_ref[...], (((1,), (1,)), ((), ())),
                          preferred_element_type=F32)
    act = jax.nn.gelu(hid)
    cols = []
    for c0 in range(0, tm, PEER_LANES):
        lanes = pl.ds(c0, PEER_LANES)
        cuts = [bmax_ref[h, i1_rows, lanes].astype(BF16) for h in range(P_HEADS)]
        e1s = [e1_ref[h, i1_rows, lanes].astype(BF16) for h in range(P_HEADS)]
        gs = []
        for r in range(PEER_I1_PER_STEP):
            g = jnp.zeros((P_NKEYS, PEER_LANES), BF16)
            for h in range(P_HEADS):
                g = jnp.where(rank2_ref[h, :, lanes] <= cuts[h][r:r + 1, :],
                              g + e1s[h][r:r + 1, :] * e2_ref[h, :, lanes], g)
            gs.append(g.astype(F32))
        cols.append(jnp.concatenate(gs, axis=0))
    gate = jnp.concatenate(cols, axis=1)
    w = (act * gate).astype(BF16)
    acc_ref[...] += jnp.dot(vt_ref[...], w, preferred_element_type=F32)

    @pl.when(e == pl.num_programs(1) - 1)
    def _():
        o_ref[...] = xres_ref[...] + g2_ref[0] * acc_ref[...].T


def peer(x3, norm_g, sc2, sh2, g2, wq, sub_keys, u_tab, v_tab):
    B, T, D = x3.shape
    N = B * T
    x = x3.reshape(N, D)
    n_hp = P_HEADS * 2
    n_exp = u_tab.shape[0]
    keys_b = sub_keys.reshape(n_hp, P_NKEYS, P_DHALF).astype(BF16)
    route_bidx = _mod_index(sc2.shape[0], PEER_ROUTE_TM, T)
    route_mod_spec = pl.BlockSpec((1, 1, D), lambda i: (route_bidx(i), 0, 0))
    mix_bidx = _mod_index(g2.shape[0], PEER_TM, T)
    route_tab = pl.BlockSpec((P_HEADS, P_NKEYS, PEER_ROUTE_TM), lambda i: (0, 0, i))
    e1, bmax, rank2, e2, xb = pl.pallas_call(
        _peer_route_kernel,
        grid=(N // PEER_ROUTE_TM,),
        in_specs=[pl.BlockSpec((PEER_ROUTE_TM, D), lambda i: (i, 0)),
                  pl.BlockSpec((1, D), lambda i: (0, 0)),
                  route_mod_spec, route_mod_spec,
                  pl.BlockSpec((D, P_HEADS * P_DKEY), lambda i: (0, 0)),
                  pl.BlockSpec((n_hp, P_NKEYS, P_DHALF), lambda i: (0, 0, 0))],
        out_specs=[route_tab, route_tab, route_tab, route_tab,
                   pl.BlockSpec((PEER_ROUTE_TM, D), lambda i: (i, 0))],
        out_shape=[jax.ShapeDtypeStruct((P_HEADS, P_NKEYS, N), F32),
                   jax.ShapeDtypeStruct((P_HEADS, P_NKEYS, N), F32),
                   jax.ShapeDtypeStruct((P_HEADS, P_NKEYS, N), BF16),
                   jax.ShapeDtypeStruct((P_HEADS, P_NKEYS, N), BF16),
                   jax.ShapeDtypeStruct((N, D), BF16)],
        compiler_params=pltpu.CompilerParams(
            dimension_semantics=("arbitrary",), vmem_limit_bytes=V7X_VMEM_LIMIT_BYTES),
        name="peer_route",
    )(x, norm_g.reshape(1, D), sc2, sh2, wq.astype(BF16), keys_b)

    mix_tab = pl.BlockSpec((P_HEADS, P_NKEYS, PEER_TM), lambda j, e: (0, 0, j))
    out = pl.pallas_call(
        _peer_mix_kernel,
        grid=(N // PEER_TM, n_exp // PEER_TE),
        in_specs=[pl.BlockSpec((PEER_TM, D), lambda j, e: (j, 0)),
                  mix_tab, mix_tab, mix_tab, mix_tab,
                  pl.BlockSpec((PEER_TE, D), lambda j, e: (e, 0)),
                  pl.BlockSpec((D, PEER_TE), lambda j, e: (0, e)),
                  pl.BlockSpec((PEER_TM, D), lambda j, e: (j, 0)),
                  pl.BlockSpec((1, 1, D), lambda j, e: (mix_bidx(j), 0, 0))],
        out_specs=pl.BlockSpec((PEER_TM, D), lambda j, e: (j, 0)),
        out_shape=jax.ShapeDtypeStruct((N, D), F32),
        scratch_shapes=[pltpu.VMEM((D, PEER_TM), F32)],
        compiler_params=pltpu.CompilerParams(
            dimension_semantics=("arbitrary", "arbitrary"), vmem_limit_bytes=V7X_VMEM_LIMIT_BYTES),
        name="peer_mix",
    )(xb, e1, bmax, rank2, e2, u_tab.astype(BF16), v_tab.astype(BF16).T, x, g2)
    return out.reshape(B, T, D)


PROJ_COL_TILE = 512


def _proj_layout():
    width = dict(zip(IN_NAMES, IN_SPLITS))
    start = dict(zip(IN_NAMES, np.cumsum((0,) + IN_SPLITS[:-1]).tolist()))
    perm, offs, pos = [], {}, 0
    for name in PROJ_ORDER:
        offs[name] = pos
        perm.extend(range(start[name], start[name] + width[name]))
        pos += width[name]
    return np.asarray(perm, np.int32), offs, pos + (-pos) % PROJ_COL_TILE


def trunk_layer(x, cvec, lp, ctx):
    B, T, _ = x.shape
    mod = jax.nn.silu(cvec) @ lp['w_mod'] + lp['b_mod']
    sh1, sc1, g1, sh2, sc2, g2 = [m[:, None, :] for m in jnp.split(mod, N_MOD, axis=-1)]
    perm, offs, n_proj = _proj_layout()
    w_in = jnp.pad(lp['w_in'][:, perm].astype(BF16), ((0, 0), (0, n_proj - len(perm))))
    proj = norm_proj(x.reshape(B * T, D_MODEL), lp['norm1_g'].reshape(1, D_MODEL), sc1, sh1, w_in, T)
    proj3 = proj.reshape(B, T, n_proj)
    (ar, ak, av, aw, aa, ag, bq, bk, bv, b_alpha, b_beta, bz, cq, ck, cv, gl) = [
        proj3[:, :, offs[n]:offs[n] + IN_SPLITS[IN_NAMES.index(n)]] for n in IN_NAMES]
    if ctx is None:
        s_rwkv0 = jnp.zeros((B, 2, A_HEADS, A_HD, A_HD), F32)
        s_gdn0 = jnp.zeros((B, 2, B_HEADS, B_HD, B_HD), F32)
        ctx_kv = None
    else:
        s_rwkv0 = ctx['rwkv'].astype(F32)
        s_gdn0 = ctx['gdn'].astype(F32)
        ctx_kv = (ctx['k'], ctx['v'])
    cols = {n: offs[n] // IN_SPLITS[IN_NAMES.index(n)] for n in ('ar', 'ak', 'av', 'aw', 'aa', 'ag')}
    ya, s_rwkv = rwkv7_branch(proj3, cols, lp, s_rwkv0)
    yb, s_gdn = gdn_branch(bq, bk, bv, b_alpha, b_beta, bz, lp, s_gdn0)
    yc, k_c, v_c = attention_branch(cq, ck, cv, lp, ctx_kv)
    w_br = jnp.stack([lp['w_br_a'], lp['w_br_b'], lp['w_br_c']]).astype(BF16)
    x2 = merge_out(ya.reshape(B * T, A_W), yb.reshape(B * T, B_W), yc.reshape(B * T, C_W), proj,
                   offs['gl'] // D_MODEL, x.reshape(B * T, D_MODEL), g1, w_br, lp['w_out'].astype(BF16), T)
    x = peer(x2.reshape(B, T, D_MODEL), lp['norm2_g'], sc2, sh2, g2,
             lp['pk_wq'], lp['pk_keys'], lp['pk_u'], lp['pk_v'])
    return x, k_c, v_c, s_rwkv, s_gdn


_PARAM_NAMES = ('w_mod', 'b_mod', 'norm1_g', 'w_in',
                'rk_w_up', 'rk_w0', 'rk_a_up', 'rk_a0', 'rk_g_up', 'rk_kk', 'rk_ka', 'rk_rk', 'rk_ln_g', 'rk_ln_b',
                'gd_conv', 'gd_a_log', 'gd_dt_bias', 'gd_norm_g',
                'at_q_g', 'at_k_g',
                'w_br_a', 'w_br_b', 'w_br_c', 'w_out',
                'norm2_g', 'pk_wq', 'pk_keys', 'pk_u', 'pk_v')


def kernel(x_prompt, x_sample, cache_attn_k, cache_attn_v, state_rwkv, state_gdn, c, c_ctx, w_mod, b_mod, norm1_g, w_in, rk_w_up, rk_w0, rk_a_up, rk_a0, rk_g_up, rk_kk, rk_ka, rk_rk, rk_ln_g, rk_ln_b, gd_conv, gd_a_log, gd_dt_bias, gd_norm_g, at_q_g, at_k_g, w_br_a, w_br_b, w_br_c, w_out, norm2_g, pk_wq, pk_keys, pk_u, pk_v, final_g):
    stacked = dict(zip(_PARAM_NAMES, (
        w_mod, b_mod, norm1_g, w_in,
        rk_w_up, rk_w0, rk_a_up, rk_a0, rk_g_up, rk_kk, rk_ka, rk_rk, rk_ln_g, rk_ln_b,
        gd_conv, gd_a_log, gd_dt_bias, gd_norm_g,
        at_q_g, at_k_g,
        w_br_a, w_br_b, w_br_c, w_out,
        norm2_g, pk_wq, pk_keys, pk_u, pk_v)))
    xp = x_prompt
    xs = x_sample
    k_list, v_list, sr_list, sg_list = [], [], [], []
    for l in range(DEPTH):
        lp = {name: arr[l] for name, arr in stacked.items()}
        xp, k_c, v_c, s_r, s_g = trunk_layer(xp, c_ctx[None, :], lp, None)
        k_list.append(k_c)
        v_list.append(v_c)
        sr_list.append(s_r)
        sg_list.append(s_g)
        ctx = {'k': cache_attn_k[:, l], 'v': cache_attn_v[:, l], 'rwkv': state_rwkv[:, l], 'gdn': state_gdn[:, l]}
        xs = trunk_layer(xs, c, lp, ctx)[0]
    y_prompt = rms_norm(xp, final_g)
    y_sample = rms_norm(xs, final_g)
    return (y_prompt, y_sample, jnp.stack(k_list, axis=1), jnp.stack(v_list, axis=1),
            jnp.stack(sr_list, axis=1).astype(x_prompt.dtype), jnp.stack(sg_list, axis=1).astype(x_prompt.dtype))
```

```python
import jax
import jax.numpy as jnp
import numpy as np
from jax import lax
from jax.experimental import pallas as pl
from jax.experimental.pallas import tpu as pltpu

F32 = jnp.float32
BF16 = jnp.bfloat16

D_MODEL = 1024
DEPTH = 2
GRID_W = 64
N_MOD = 6
EPS = 1e-6
GN_EPS = 64e-5
A_HEADS = 8
A_HD = 64
A_W = A_HEADS * A_HD
A_LORA_W = 64
A_LORA_A = 64
A_LORA_G = 128
A_DECAY_SCALE = 0.606531
B_HEADS = 4
B_HD = 128
B_W = B_HEADS * B_HD
B_CONV = 5
C_HEADS = 8
C_KV = 2
C_HD = 64
C_W = C_HEADS * C_HD
C_KVW = C_KV * C_HD
ROPE_BASE = 10000.0
N_BRANCH = 3
IN_SPLITS = (A_W, A_W, A_W, 2 * A_LORA_W, 2 * A_LORA_A, A_LORA_G,
             B_W, B_W, B_W, 2 * B_HEADS, 2 * B_HEADS, B_W,
             C_W, C_KVW, C_KVW, N_BRANCH * D_MODEL)
IN_NAMES = ('ar', 'ak', 'av', 'aw', 'aa', 'ag', 'bq', 'bk', 'bv', 'b_alpha', 'b_beta', 'bz', 'cq', 'ck', 'cv', 'gl')
PROJ_ORDER = ('gl', 'ar', 'ak', 'av', 'bq', 'bk', 'bv', 'bz', 'cq', 'ck', 'cv', 'aw', 'aa', 'ag', 'b_alpha', 'b_beta')
P_HEADS = 8
P_NKEYS = 128
P_TOPK = 16
P_DKEY = 256
P_DHALF = P_DKEY // 2

V7X_VMEM_LIMIT_BYTES = 48 * 1024 * 1024
LANE = 128
NEG_INF = float('-inf')
HIGHEST = lax.Precision.HIGHEST


def _mm_kernel(x_ref, w_ref, o_ref):
    o_ref[...] = jnp.dot(x_ref[...].astype(BF16), w_ref[...], preferred_element_type=F32)


def _pick_tile(n, candidates):
    for c in candidates:
        if n % c == 0:
            return c
    return n


def pmm(x, w):
    M, K = x.shape
    N = w.shape[1]
    n_pad = (-N) % LANE
    wb = w.astype(BF16)
    if n_pad:
        wb = jnp.pad(wb, ((0, 0), (0, n_pad)))
    Np = N + n_pad
    tm = _pick_tile(M, (1024, 512, 256, 128, 64, 32, 16, 8))
    tn = _pick_tile(Np, (512, 256, 128))
    out = pl.pallas_call(
        _mm_kernel,
        grid=(M // tm, Np // tn),
        in_specs=[pl.BlockSpec((tm, K), lambda i, j: (i, 0)),
                  pl.BlockSpec((K, tn), lambda i, j: (0, j))],
        out_specs=pl.BlockSpec((tm, tn), lambda i, j: (i, j)),
        out_shape=jax.ShapeDtypeStruct((M, Np), F32),
        compiler_params=pltpu.CompilerParams(
            dimension_semantics=("arbitrary", "arbitrary"),
            vmem_limit_bytes=V7X_VMEM_LIMIT_BYTES),
        name="pmm",
    )(x, wb)
    return out[:, :N] if n_pad else out


def pmm3(x, w):
    B, T, K = x.shape
    return pmm(x.reshape(B * T, K), w).reshape(B, T, w.shape[1])


def _mod_norm(x, g, sc, sh):
    y = x * lax.rsqrt(jnp.mean(x * x, axis=-1, keepdims=True) + EPS)
    return (y * g) * (1.0 + sc) + sh


def _mod_index(n_mod, rows_per_step, rows_per_batch):
    if n_mod == 1:
        return lambda i: 0
    assert rows_per_batch % rows_per_step == 0
    return lambda i: (i * rows_per_step) // rows_per_batch


def _norm_proj_kernel(x_ref, g_ref, sc_ref, sh_ref, w_ref, o_ref, h_ref):
    @pl.when(pl.program_id(1) == 0)
    def _():
        h_ref[...] = _mod_norm(x_ref[...], g_ref[...], sc_ref[0], sh_ref[0]).astype(BF16)

    o_ref[...] = jnp.dot(h_ref[...], w_ref[...], preferred_element_type=F32)


def norm_proj(x, g, sc, sh, w, rows_per_batch):
    N, D = x.shape
    Np = w.shape[1]
    tm = _pick_tile(N, (1024, 512, 256))
    tn = _pick_tile(Np, (512, 256, 128))
    bidx = _mod_index(sc.shape[0], tm, rows_per_batch)
    mod_spec = pl.BlockSpec((1, 1, D), lambda i, j: (bidx(i), 0, 0))
    return pl.pallas_call(
        _norm_proj_kernel,
        grid=(N // tm, Np // tn),
        in_specs=[pl.BlockSpec((tm, D), lambda i, j: (i, 0)),
                  pl.BlockSpec((1, D), lambda i, j: (0, 0)),
                  mod_spec, mod_spec,
                  pl.BlockSpec((D, tn), lambda i, j: (0, j))],
        out_specs=pl.BlockSpec((tm, tn), lambda i, j: (i, j)),
        out_shape=jax.ShapeDtypeStruct((N, Np), F32),
        scratch_shapes=[pltpu.VMEM((tm, D), BF16)],
        compiler_params=pltpu.CompilerParams(
            dimension_semantics=("arbitrary", "arbitrary"), vmem_limit_bytes=V7X_VMEM_LIMIT_BYTES),
        name="norm_proj",
    )(x, g, sc, sh, w)


MERGE_TM = 256


def _merge_out_kernel(ya_ref, yb_ref, yc_ref, ga_ref, gb_ref, gc_ref, x_ref, g1_ref, wbr_ref, wout_ref, o_ref):
    def branch(y_ref, gate_ref, i):
        return jax.nn.sigmoid(gate_ref[...]) * jnp.dot(y_ref[...].astype(BF16), wbr_ref[i], preferred_element_type=F32)

    merged = branch(ya_ref, ga_ref, 0) + branch(yb_ref, gb_ref, 1) + branch(yc_ref, gc_ref, 2)
    o_ref[...] = x_ref[...] + g1_ref[0] * jnp.dot(merged.astype(BF16), wout_ref[...], preferred_element_type=F32)


def merge_out(ya, yb, yc, proj, gate_col_block, x, g1, w_br, w_out, rows_per_batch):
    N, D = x.shape
    W = ya.shape[1]
    tm = MERGE_TM
    bidx = _mod_index(g1.shape[0], tm, rows_per_batch)
    y_spec = pl.BlockSpec((tm, W), lambda i: (i, 0))

    def gate_spec(k):
        return pl.BlockSpec((tm, D), lambda i: (i, gate_col_block + k))

    return pl.pallas_call(
        _merge_out_kernel,
        grid=(N // tm,),
        in_specs=[y_spec, y_spec, y_spec, gate_spec(0), gate_spec(1), gate_spec(2),
                  pl.BlockSpec((tm, D), lambda i: (i, 0)),
                  pl.BlockSpec((1, 1, D), lambda i: (bidx(i), 0, 0)),
                  pl.BlockSpec((N_BRANCH, W, D), lambda i: (0, 0, 0)),
                  pl.BlockSpec((D, D), lambda i: (0, 0))],
        out_specs=pl.BlockSpec((tm, D), lambda i: (i, 0)),
        out_shape=jax.ShapeDtypeStruct((N, D), F32),
        compiler_params=pltpu.CompilerParams(
            dimension_semantics=("arbitrary",), vmem_limit_bytes=V7X_VMEM_LIMIT_BYTES),
        name="merge_out",
    )(ya, yb, yc, proj, proj, proj, x, g1, w_br, w_out)


def rms_norm(x, g):
    x32 = x.astype(F32)
    y = x32 * lax.rsqrt(jnp.mean(x32 * x32, axis=-1, keepdims=True) + EPS)
    return (y * g.astype(F32)).astype(x.dtype)


def l2_normalize(x):
    x32 = x.astype(F32)
    return x32 * lax.rsqrt(jnp.sum(x32 * x32, axis=-1, keepdims=True) + EPS)


def split_cols(p, sizes):
    cuts = np.cumsum(np.asarray(sizes))[:-1].tolist()
    return jnp.split(p, cuts, axis=-1)


def axial_rope(x):
    B, T, H, D = x.shape
    n_rows = T // GRID_W
    n_freq = D // 4
    rows = jnp.repeat(jnp.arange(n_rows, dtype=F32), GRID_W)
    cols = jnp.tile(jnp.arange(GRID_W, dtype=F32), n_rows)
    inv = ROPE_BASE ** (-jnp.arange(n_freq, dtype=F32) / n_freq)
    ang = jnp.stack([rows[:, None] * inv, cols[:, None] * inv], axis=1)
    cos = jnp.cos(ang)[None, :, None]
    sin = jnp.sin(ang)[None, :, None]
    xr = x.astype(F32).reshape(B, T, H, 2, 2, n_freq)
    x1 = xr[..., 0, :]
    x2 = xr[..., 1, :]
    out = jnp.stack([x1 * cos - x2 * sin, x2 * cos + x1 * sin], axis=-2)
    return out.reshape(B, T, H, D).astype(x.dtype)


ATTN_TQ = 256


def _attn_kernel(q_ref, kt_ref, v_ref, o_ref):
    tq = q_ref.shape[1]
    n_kv, hd = kt_ref.shape[1], kt_ref.shape[2]
    grp = q_ref.shape[2] // (n_kv * hd)
    scale = hd ** -0.5
    q_all = (q_ref[0].astype(BF16) * scale).astype(BF16)
    outs = []
    for g in range(n_kv):
        q = jnp.concatenate([q_all[:, (g * grp + j) * hd:(g * grp + j + 1) * hd] for j in range(grp)], axis=0)
        s = jnp.dot(q, kt_ref[0, g], preferred_element_type=F32)
        p = jnp.exp(s - jnp.max(s, axis=-1, keepdims=True))
        p = (p * (1.0 / jnp.sum(p, axis=-1, keepdims=True))).astype(BF16)
        o = jnp.dot(p, v_ref[0, g], preferred_element_type=F32)
        outs += [o[j * tq:(j + 1) * tq] for j in range(grp)]
    o_ref[0] = jnp.concatenate(outs, axis=1)


def blocked_attention(q, k, v):
    B, T, W = q.shape
    S, KV, D = k.shape[1], k.shape[2], k.shape[3]
    assert D == 4 ** (D.bit_length() // 2), "the kernel folds D ** -0.5 into bf16 q: it must be a power of two"
    tq = min(ATTN_TQ, T)
    kt = jnp.transpose(k, (0, 2, 3, 1)).astype(BF16)
    vh = jnp.transpose(v, (0, 2, 1, 3)).astype(BF16)
    return pl.pallas_call(
        _attn_kernel,
        grid=(B, T // tq),
        in_specs=[pl.BlockSpec((1, tq, W), lambda b, i: (b, i, 0)),
                  pl.BlockSpec((1, KV, D, S), lambda b, i: (b, 0, 0, 0)),
                  pl.BlockSpec((1, KV, S, D), lambda b, i: (b, 0, 0, 0))],
        out_specs=pl.BlockSpec((1, tq, W), lambda b, i: (b, i, 0)),
        out_shape=jax.ShapeDtypeStruct((B, T, W), F32),
        compiler_params=pltpu.CompilerParams(
            dimension_semantics=("arbitrary", "arbitrary"), vmem_limit_bytes=V7X_VMEM_LIMIT_BYTES),
        name="gqa_attention",
    )(q, kt, vh)


RWKV_CHUNK = 128


def _mm(a, b):
    return jnp.dot(a.astype(BF16), b.astype(BF16), preferred_element_type=F32)


def _mm_nt(a, b):
    return lax.dot_general(a.astype(BF16), b.astype(BF16), (((1,), (1,)), ((), ())), preferred_element_type=F32)


def _mm_tn(a, b):
    return lax.dot_general(a.astype(BF16), b.astype(BF16), (((0,), (0,)), ((), ())), preferred_element_type=F32)


TRI_BASE = 8


def _unit_tri_inverses(l_mats):
    n = l_mats[0].shape[0]
    row = lax.broadcasted_iota(jnp.int32, (n, n), 0)
    col = lax.broadcasted_iota(jnp.int32, (n, n), 1)

    def same_block(b):
        shift = b.bit_length() - 1
        return jnp.right_shift(row, shift) == jnp.right_shift(col, shift)

    base = same_block(TRI_BASE)
    eye = jnp.where(row == col, 1.0, 0.0)
    ps = [jnp.where(base, l, 0.0) for l in l_mats]
    xs = [eye + p for p in ps]
    span = 2
    while span < TRI_BASE:
        ps = [_mm(p, p) for p in ps]
        xs = [x + _mm(x, p) for x, p in zip(xs, ps)]
        span *= 2
    b = TRI_BASE
    while b < n:
        pair = same_block(2 * b) & jnp.logical_not(same_block(b))
        ts = [_mm(x, jnp.where(pair, l, 0.0)) for x, l in zip(xs, l_mats)]
        xs = [x + _mm(t, x) for x, t in zip(xs, ts)]
        b *= 2
    return xs


def _head_sum(x, ones_bd):
    hi = x.astype(BF16)
    rest = x - hi.astype(F32)
    mid = rest.astype(BF16)
    lo = (rest - mid.astype(F32)).astype(BF16)
    return (jnp.dot(hi, ones_bd, preferred_element_type=F32)
            + (jnp.dot(mid, ones_bd, preferred_element_type=F32) + jnp.dot(lo, ones_bd, preferred_element_type=F32)))


def _rwkv_gate_a(pa, aup, a0):
    return jax.nn.sigmoid(a0 + jnp.dot(pa.astype(BF16), aup, preferred_element_type=F32))


def _rwkv_chunk_kernel(incl_ref, strict_ref, ones_ref, kk_ref, ka_ref, wup_ref, w0_ref, aup_ref, a0_ref,
                       r_ref, k_ref, v_ref, pw_ref, pa_ref, s0_ref, y_ref, sfin_ref, s_ref):
    c = pl.program_id(2)

    @pl.when(c == 0)
    def _():
        s_ref[...] = s0_ref[0, 0]

    incl = incl_ref[0]
    strict = strict_ref[0]
    n_heads, hd = s_ref.shape[0], s_ref.shape[1]
    heads = range(n_heads)
    r = r_ref[0]
    v = v_ref[0]
    lw = -A_DECAY_SCALE * jax.nn.sigmoid(
        w0_ref[0] + jnp.dot(jnp.tanh(pw_ref[0]).astype(BF16), wup_ref[0], preferred_element_type=F32))
    a = _rwkv_gate_a(pa_ref[0], aup_ref[0], a0_ref[0])
    kq = k_ref[0] * kk_ref[...]
    kk = kq * lax.rsqrt(_head_sum(kq * kq, ones_ref[...]) + EPS)
    k_d = k_ref[0] * (1.0 + (a - 1.0) * ka_ref[...])
    bb = kk * a
    lam = jnp.dot(incl, lw, precision=HIGHEST, preferred_element_type=F32)
    lam_end = jnp.sum(lw, axis=0, keepdims=True)
    e_neg = jnp.exp(-lam)
    e_end = jnp.exp(lam_end - lam)
    s_scale = jnp.exp(lam_end)
    ar_all = jnp.concatenate([-kk * jnp.exp(lam - lw), r * jnp.exp(lam)], axis=0).astype(BF16)
    bk_all = jnp.concatenate([bb * e_neg, k_d * e_neg], axis=0).astype(BF16)
    v_all = v.astype(BF16)
    be_all = (bb * e_end).astype(BF16)
    ke_all = (k_d * e_end).astype(BF16)
    cs = r.shape[0]

    def head(x, h):
        return x[:, h * hd:(h + 1) * hd]

    s_old = [s_ref[h] for h in heads]
    ars = [head(ar_all, h) for h in heads]
    vs = [head(v_all, h) for h in heads]
    ps = [_mm_nt(ars[h], s_old[h]) for h in heads]
    ms = [_mm_nt(ars[h], head(bk_all, h)) for h in heads]
    rhs = [ps[h][:cs] + _mm(ms[h][:cs, cs:] * strict, vs[h]) for h in heads]
    invs = _unit_tri_inverses([ms[h][:cs, :cs] * strict for h in heads])
    us = [_mm(invs[h], rhs[h]) for h in heads]
    ys = [ps[h][cs:] + _mm(ms[h][cs:, :cs] * incl, us[h]) + _mm(ms[h][cs:, cs:] * incl, vs[h]) for h in heads]
    y_ref[0, 0] = jnp.concatenate(ys, axis=1)
    for h in heads:
        s_ref[h] = (s_old[h] * head(s_scale, h) + _mm_tn(us[h], head(be_all, h)) + _mm_tn(vs[h], head(ke_all, h)))

    @pl.when(c == pl.num_programs(2) - 1)
    def _():
        sfin_ref[0, 0] = s_ref[...]


def _order_masks(n):
    idx = jnp.arange(n)
    incl = jnp.stack([idx[:, None] >= idx[None, :], idx[:, None] <= idx[None, :]]).astype(F32)
    strict = jnp.stack([idx[:, None] > idx[None, :], idx[:, None] < idx[None, :]]).astype(F32)
    return incl, strict


def _lora_rows(w_up):
    z = jnp.zeros_like(w_up[0])
    return jnp.stack([jnp.concatenate([w_up[0], z], axis=0), jnp.concatenate([z, w_up[1]], axis=0)]).astype(BF16)


def rwkv_scan(proj, cols, lp, s0):
    B, T, _ = proj.shape
    H, N = s0.shape[2], s0.shape[3]
    W = H * N
    C = RWKV_CHUNK
    nc = T // C
    incl, strict = _order_masks(C)

    def chunk(d, c):
        return c + d * (nc - 1 - 2 * c)

    def col_spec(j, width):
        return pl.BlockSpec((1, C, width), lambda b, d, c: (b, chunk(d, c), j))

    def full(shape):
        return pl.BlockSpec(shape, lambda b, d, c: (0,) * len(shape))

    def per_dir(shape):
        return pl.BlockSpec((1,) + shape, lambda b, d, c: (d,) + (0,) * len(shape))

    lora = 2 * A_LORA_W
    dir_spec = pl.BlockSpec((1, 1, C, W), lambda b, d, c: (d, b, chunk(d, c), 0))
    state_spec = pl.BlockSpec((1, 1, H, N, N), lambda b, d, c: (b, d, 0, 0, 0))
    return pl.pallas_call(
        _rwkv_chunk_kernel,
        grid=(B, 2, nc),
        in_specs=[per_dir((C, C)), per_dir((C, C)), full((W, W)), full((1, W)), full((1, W)),
                  per_dir((lora, W)), per_dir((1, W)), per_dir((lora, W)), per_dir((1, W)),
                  col_spec(cols['ar'], W), col_spec(cols['ak'], W), col_spec(cols['av'], W),
                  col_spec(cols['aw'], lora), col_spec(cols['aa'], lora), state_spec],
        out_specs=[dir_spec, state_spec],
        out_shape=[jax.ShapeDtypeStruct((2, B, T, W), F32),
                   jax.ShapeDtypeStruct((B, 2, H, N, N), F32)],
        scratch_shapes=[pltpu.VMEM((H, N, N), F32)],
        compiler_params=pltpu.CompilerParams(
            dimension_semantics=("arbitrary", "arbitrary", "arbitrary"), vmem_limit_bytes=V7X_VMEM_LIMIT_BYTES),
        name="rwkv_chunk",
    )(incl, strict, _head_ones(H, N), lp['rk_kk'].reshape(1, W), lp['rk_ka'].reshape(1, W),
      _lora_rows(lp['rk_w_up']), lp['rk_w0'].reshape(2, 1, W), _lora_rows(lp['rk_a_up']), lp['rk_a0'].reshape(2, 1, W),
      proj, proj, proj, proj, proj, s0)


def _head_ones(n_heads, hd):
    hid = jnp.arange(n_heads * hd) // hd
    return (hid[:, None] == hid[None, :]).astype(BF16)


RWKV_POST_T = 256


def _rwkv_post_kernel(ones_ref, ka_ref, rk_ref, lng_ref, lnb_ref, aup_ref, a0_ref, gup_ref,
                      y0_ref, y1_ref, r_ref, k_ref, v_ref, pa_ref, pg_ref, o_ref):
    ones_bd = ones_ref[...]
    r = r_ref[0]
    k = k_ref[0]
    v = v_ref[0]
    y = y0_ref[0, 0] + y1_ref[0, 0]
    for d in range(2):
        k_d = k * (1.0 + (_rwkv_gate_a(pa_ref[0], aup_ref[d], a0_ref[d]) - 1.0) * ka_ref[...])
        y = y + _head_sum(r * k_d * rk_ref[...], ones_bd) * v
    inv_n = 1.0 / (ones_bd.shape[0] // A_HEADS)
    yc = y - _head_sum(y, ones_bd) * inv_n
    var = _head_sum(yc * yc, ones_bd) * inv_n
    yn = yc * lax.rsqrt(var + GN_EPS) * lng_ref[...] + lnb_ref[...]
    gate = jnp.dot(jax.nn.sigmoid(pg_ref[0]).astype(BF16), gup_ref[...], preferred_element_type=F32)
    o_ref[0] = yn * gate


def rwkv_post(y_dir, proj, cols, lp):
    _, B, T, W = y_dir.shape
    tp = min(RWKV_POST_T, T)
    lora = 2 * A_LORA_A

    def col_spec(j, width):
        return pl.BlockSpec((1, tp, width), lambda b, i: (b, i, j))

    def full(shape):
        return pl.BlockSpec(shape, lambda b, i: (0,) * len(shape))

    def y_spec(d):
        return pl.BlockSpec((1, 1, tp, W), lambda b, i: (d, b, i, 0))

    return pl.pallas_call(
        _rwkv_post_kernel,
        grid=(B, T // tp),
        in_specs=[full((W, W)), full((1, W)), full((1, W)), full((1, W)), full((1, W)),
                  full((2, lora, W)), full((2, 1, W)), full((A_LORA_G, W)),
                  y_spec(0), y_spec(1),
                  col_spec(cols['ar'], W), col_spec(cols['ak'], W), col_spec(cols['av'], W),
                  col_spec(cols['aa'], lora), col_spec(cols['ag'], A_LORA_G)],
        out_specs=pl.BlockSpec((1, tp, W), lambda b, i: (b, i, 0)),
        out_shape=jax.ShapeDtypeStruct((B, T, W), F32),
        compiler_params=pltpu.CompilerParams(
            dimension_semantics=("arbitrary", "arbitrary"), vmem_limit_bytes=V7X_VMEM_LIMIT_BYTES),
        name="rwkv_post",
    )(_head_ones(A_HEADS, A_HD), lp['rk_ka'].reshape(1, W), lp['rk_rk'].reshape(1, W),
      lp['rk_ln_g'].reshape(1, W), lp['rk_ln_b'].reshape(1, W),
      _lora_rows(lp['rk_a_up']), lp['rk_a0'].reshape(2, 1, W), lp['rk_g_up'].astype(BF16),
      y_dir, y_dir, proj, proj, proj, proj, proj)


def rwkv7_branch(proj, cols, lp, s0):
    y_dir, s_fin = rwkv_scan(proj, cols, lp, s0)
    return rwkv_post(y_dir, proj, cols, lp), s_fin


def short_conv(x, w):
    T = x.shape[1]
    half = B_CONV // 2
    xp = jnp.pad(x, ((0, 0), (half, half), (0, 0)))
    y = xp[:, 0:T, :] * w[0].astype(x.dtype)
    for j in range(1, B_CONV):
        y = y + xp[:, j:j + T, :] * w[j].astype(x.dtype)
    return jax.nn.silu(y)


GDN_CHUNK = 128


def _gdn_chunk_kernel(incl_ref, strict_ref, q_ref, k_ref, v_ref, g_ref, beta_ref, s0_ref, o_ref, sfin_ref, s_ref):
    c = pl.program_id(2)

    @pl.when(c == 0)
    def _():
        s_ref[...] = s0_ref[0, 0]

    incl = incl_ref[0]
    strict = strict_ref[0]
    cs = incl.shape[0]
    eye = (lax.broadcasted_iota(jnp.int32, (cs, cs), 0) == lax.broadcasted_iota(jnp.int32, (cs, cs), 1)).astype(F32)
    hd = s_ref.shape[1]
    heads = range(s_ref.shape[0])

    def head(ref, h):
        return ref[0, :, h * hd:(h + 1) * hd]

    g_rows = [g_ref[0, 0, 0, h:h + 1, :] for h in heads]
    gc_rows = [lax.dot_general(g, incl, (((1,), (1,)), ((), ())), precision=HIGHEST, preferred_element_type=F32)
               for g in g_rows]
    gc_cols = [jnp.sum(eye * g, axis=1, keepdims=True) for g in gc_rows]
    beta_cols = [jnp.sum(eye * beta_ref[0, 0, 0, h:h + 1, :], axis=1, keepdims=True) for h in heads]
    g_ends = [jnp.sum(g, axis=1, keepdims=True) for g in g_rows]
    decays = [jnp.exp(jnp.where(incl > 0, gc_cols[h] - gc_rows[h], NEG_INF)) for h in heads]
    ks = [head(k_ref, h) for h in heads]
    qs = [head(q_ref, h) for h in heads]
    kbs = [ks[h] * beta_cols[h] for h in heads]
    k_bf = [k.astype(BF16) for k in ks]
    lmats = [strict * _mm_nt(kbs[h], k_bf[h]) * decays[h] for h in heads]
    aqks = [incl * _mm_nt(qs[h], k_bf[h]) * decays[h] for h in heads]
    tinvs = _unit_tri_inverses([-l for l in lmats])
    s_old = [s_ref[h] for h in heads]
    us = [_mm(tinvs[h], head(v_ref, h) * beta_cols[h]) for h in heads]
    wks = [_mm(tinvs[h], kbs[h] * jnp.exp(gc_cols[h])) for h in heads]
    v_new = [us[h] - _mm(wks[h], s_old[h]) for h in heads]
    outs = [_mm(qs[h] * jnp.exp(gc_cols[h]), s_old[h]) + _mm(aqks[h], v_new[h]) for h in heads]
    o_ref[0, 0] = jnp.concatenate(outs, axis=1)
    for h in heads:
        s_ref[h] = s_old[h] * jnp.exp(g_ends[h]) + _mm_tn(ks[h] * jnp.exp(g_ends[h] - gc_cols[h]), v_new[h])

    @pl.when(c == pl.num_programs(2) - 1)
    def _():
        sfin_ref[0, 0] = s_ref[...]


def gdn_scan(qkv, g, beta, s0):
    B, T, _ = qkv.shape
    H, D = B_HEADS, B_HD
    W = H * D
    C = GDN_CHUNK
    nc = T // C
    incl, strict = _order_masks(C)

    def rows(t):
        return jnp.swapaxes(t.reshape(2, B, nc, C, H), -1, -2)

    def chunk(d, c):
        return c + d * (nc - 1 - 2 * c)

    def seq_spec(j):
        return pl.BlockSpec((1, C, W), lambda b, d, c: (b, chunk(d, c), j))

    row_spec = pl.BlockSpec((1, 1, 1, H, C), lambda b, d, c: (d, b, chunk(d, c), 0, 0))
    state_spec = pl.BlockSpec((1, 1, H, D, D), lambda b, d, c: (b, d, 0, 0, 0))
    return pl.pallas_call(
        _gdn_chunk_kernel,
        grid=(B, 2, nc),
        in_specs=[pl.BlockSpec((1, C, C), lambda b, d, c: (d, 0, 0)),
                  pl.BlockSpec((1, C, C), lambda b, d, c: (d, 0, 0)),
                  seq_spec(0), seq_spec(1), seq_spec(2), row_spec, row_spec, state_spec],
        out_specs=[pl.BlockSpec((1, 1, C, W), lambda b, d, c: (d, b, chunk(d, c), 0)), state_spec],
        out_shape=[jax.ShapeDtypeStruct((2, B, T, W), F32),
                   jax.ShapeDtypeStruct((B, 2, H, D, D), F32)],
        scratch_shapes=[pltpu.VMEM((H, D, D), F32)],
        compiler_params=pltpu.CompilerParams(
            dimension_semantics=("arbitrary", "arbitrary", "arbitrary"), vmem_limit_bytes=V7X_VMEM_LIMIT_BYTES),
        name="gdn_chunk",
    )(incl, strict, qkv, qkv, qkv, rows(g), rows(beta), s0)


def _gdn_pre_kernel(x_ref, w_ref, o_ref):
    j = pl.program_id(1)
    x = x_ref[0]
    n = x.shape[0]
    t = lax.broadcasted_iota(jnp.int32, x.shape, 0)
    half = B_CONV // 2
    y = x * w_ref[half:half + 1, :]
    for tap in range(B_CONV):
        back = half - tap
        if back == 0:
            continue
        shifted = pltpu.roll(x, back % n, axis=0)
        valid = (t >= back) if back > 0 else (t < n + back)
        y = y + jnp.where(valid, shifted, 0.0) * w_ref[tap:tap + 1, :]
    y = y * jax.nn.sigmoid(y)
    normed = y * lax.rsqrt(jnp.sum(y * y, axis=-1, keepdims=True) + EPS)
    scale = jnp.where(j < B_HEADS, B_HD ** -0.5, 1.0)
    o_ref[0] = jnp.where(j < 2 * B_HEADS, normed * scale, y)


def gdn_pre(proj, col_block, conv_w):
    B, T, _ = proj.shape
    n_tiles = 3 * B_HEADS
    return pl.pallas_call(
        _gdn_pre_kernel,
        grid=(B, n_tiles),
        in_specs=[pl.BlockSpec((1, T, B_HD), lambda b, j: (b, 0, col_block + j)),
                  pl.BlockSpec((B_CONV, B_HD), lambda b, j: (0, j))],
        out_specs=pl.BlockSpec((1, T, B_HD), lambda b, j: (b, 0, j)),
        out_shape=jax.ShapeDtypeStruct((B, T, n_tiles * B_HD), F32),
        compiler_params=pltpu.CompilerParams(
            dimension_semantics=("arbitrary", "arbitrary"), vmem_limit_bytes=V7X_VMEM_LIMIT_BYTES),
        name="gdn_pre",
    )(proj, conv_w)


GDN_POST_T = 256


def _gdn_post_kernel(g_ref, o0_ref, o1_ref, z_ref, y_ref):
    o = o0_ref[0, 0] + o1_ref[0, 0]
    z = z_ref[0]
    outs = []
    for h in range(B_HEADS):
        lanes = slice(h * B_HD, (h + 1) * B_HD)
        oh = o[:, lanes]
        zh = z[:, lanes]
        normed = oh * lax.rsqrt(jnp.mean(oh * oh, axis=-1, keepdims=True) + EPS) * g_ref[...]
        outs.append(normed * (zh * jax.nn.sigmoid(zh)))
    y_ref[0] = jnp.concatenate(outs, axis=1)


def gdn_post(o_dir, proj, z_col_block, norm_g):
    _, B, T, W = o_dir.shape
    tp = min(GDN_POST_T, T)

    def o_spec(d):
        return pl.BlockSpec((1, 1, tp, W), lambda b, i: (d, b, i, 0))

    return pl.pallas_call(
        _gdn_post_kernel,
        grid=(B, T // tp),
        in_specs=[pl.BlockSpec((1, B_HD), lambda b, i: (0, 0)), o_spec(0), o_spec(1),
                  pl.BlockSpec((1, tp, W), lambda b, i: (b, i, z_col_block))],
        out_specs=pl.BlockSpec((1, tp, W), lambda b, i: (b, i, 0)),
        out_shape=jax.ShapeDtypeStruct((B, T, W), F32),
        compiler_params=pltpu.CompilerParams(
            dimension_semantics=("arbitrary", "arbitrary"), vmem_limit_bytes=V7X_VMEM_LIMIT_BYTES),
        name="gdn_post",
    )(norm_g.reshape(1, B_HD), o_dir, o_dir, proj)


def gdn_branch(proj, qkv_col_block, z_col_block, p_alpha, p_beta, lp, s0):
    B, T, _ = proj.shape
    qkv = gdn_pre(proj, qkv_col_block, lp['gd_conv'])
    alpha = p_alpha.astype(F32).reshape(B, T, 2, B_HEADS)
    beta_in = p_beta.astype(F32).reshape(B, T, 2, B_HEADS)
    g = jnp.stack([-jnp.exp(lp['gd_a_log'][d].astype(F32)) * jax.nn.softplus(alpha[:, :, d] + lp['gd_dt_bias'][d])
                   for d in range(2)])
    beta = jnp.stack([jax.nn.sigmoid(beta_in[:, :, d]) for d in range(2)])
    o_dir, s_fin = gdn_scan(qkv, g, beta, s0)
    return gdn_post(o_dir, proj, z_col_block, lp['gd_norm_g']), s_fin


def attention_branch(pq, pk, pv, lp, ctx_kv):
    B, T, _ = pq.shape
    q = rms_norm(pq.reshape(B, T, C_HEADS, C_HD), lp['at_q_g'])
    k = rms_norm(pk.reshape(B, T, C_KV, C_HD), lp['at_k_g'])
    v = pv.reshape(B, T, C_KV, C_HD)
    if ctx_kv is None:
        o = blocked_attention(q.reshape(B, T, C_W), k, v)
    else:
        keys = jnp.concatenate([ctx_kv[0].astype(k.dtype), axial_rope(k)], axis=1)
        vals = jnp.concatenate([ctx_kv[1].astype(v.dtype), v], axis=1)
        o = blocked_attention(axial_rope(q).reshape(B, T, C_W), keys, vals)
    return o, k, v


PEER_ROUTE_TM = 256
PEER_TM = 512
PEER_I1_PER_STEP = 8
PEER_TE = PEER_I1_PER_STEP * P_NKEYS
PEER_LANES = 128


def _top16_rows(s, with_rank=False):
    rows = []
    cur = s
    rank = jnp.full(s.shape, float(P_TOPK), F32)
    for r in range(P_TOPK):
        m = jnp.max(cur, axis=0, keepdims=True)
        rows.append(m)
        hit = cur == m
        if with_rank:
            rank = jnp.where(hit, float(r), rank)
        cur = jnp.where(hit, NEG_INF, cur)
    return (rows, rank) if with_rank else rows


def _rows_to_mat(rows):
    n = len(rows)
    tm = rows[0].shape[1]
    rid = lax.broadcasted_iota(jnp.int32, (n, tm), 0)
    mat = jnp.zeros((n, tm), F32)
    for r, row in enumerate(rows):
        mat = jnp.where(rid == r, row, mat)
    return mat


SUBLANES = 8


def _peer_route_kernel(x_ref, ng_ref, sc_ref, sh_ref, wq_ref, keys_ref, e1_ref, bmax_ref, rank2_ref, e2_ref, xb_ref):
    xb = _mod_norm(x_ref[...], ng_ref[...], sc_ref[0], sh_ref[0]).astype(BF16)
    xb_ref[...] = xb
    q = jnp.dot(xb, wq_ref[...], preferred_element_type=F32).astype(BF16)
    tm = xb.shape[0]
    rid = lax.broadcasted_iota(jnp.int32, (SUBLANES, tm), 0)
    for h in range(P_HEADS):
        s1, s2 = [lax.dot_general(keys_ref[h * 2 + p], q[:, (h * 2 + p) * P_DHALF:(h * 2 + p + 1) * P_DHALF],
                                  (((1,), (1,)), ((), ())), preferred_element_type=F32) for p in range(2)]
        t1_rows = _top16_rows(s1)
        t2_rows, rank2 = _top16_rows(s2, with_rank=True)
        t2 = _rows_to_mat(t2_rows)
        cands = [t1_rows[0] + t2]
        for a in range(1, SUBLANES):
            cands.append(jnp.where(rid < P_TOPK // (a + 1), t1_rows[a] + t2[:SUBLANES], NEG_INF))
        cands.append(_rows_to_mat(t1_rows[SUBLANES:]) + t2_rows[0])
        m0 = t1_rows[0] + t2_rows[0]
        z = jnp.zeros_like(m0)
        tau = m0
        for _ in range(P_TOPK):
            m = jnp.max(cands[0], axis=0, keepdims=True)
            for cnd in cands[1:]:
                m = jnp.maximum(m, jnp.max(cnd, axis=0, keepdims=True))
            z = z + jnp.exp(m - m0)
            tau = m
            cands = [jnp.where(cnd == m, NEG_INF, cnd) for cnd in cands]
        bmax = jnp.full(s1.shape, -1.0, F32)
        for b in range(P_TOPK):
            bmax = jnp.where(s1 + t2_rows[b] >= tau, float(b), bmax)
        e1_ref[h] = jnp.exp(s1 - t1_rows[0]) * (1.0 / z)
        bmax_ref[h] = bmax
        rank2_ref[h] = rank2.astype(BF16)
        e2_ref[h] = jnp.exp(s2 - t2_rows[0]).astype(BF16)


def _peer_mix_kernel(xb_ref, e1_ref, bmax_ref, rank2_ref, e2_ref, u_ref, vt_ref, xres_ref, g2_ref, o_ref, acc_ref):
    e = pl.program_id(1)
    tm = xb_ref.shape[0]

    @pl.when(e == 0)
    def _():
        acc_ref[...] = jnp.zeros_like(acc_ref)

    i1_rows = pl.ds(pl.multiple_of(e * PEER_I1_PER_STEP, PEER_I1_PER_STEP), PEER_I1_PER_STEP)
    hid = lax.dot_general(u_ref[...], xb_ref[...], (((1,), (1,)), ((), ())),
                          preferred_element_type=F32)
    act = jax.nn.gelu(hid)
    cols = []
    for c0 in range(0, tm, PEER_LANES):
        lanes = pl.ds(c0, PEER_LANES)
        cuts = [bmax_ref[h, i1_rows, lanes].astype(BF16) for h in range(P_HEADS)]
        e1s = [e1_ref[h, i1_rows, lanes].astype(BF16) for h in range(P_HEADS)]
        gs = []
        for r in range(PEER_I1_PER_STEP):
            g = jnp.zeros((P_NKEYS, PEER_LANES), BF16)
            for h in range(P_HEADS):
                g = jnp.where(rank2_ref[h, :, lanes] <= cuts[h][r:r + 1, :],
                              g + e1s[h][r:r + 1, :] * e2_ref[h, :, lanes], g)
            gs.append(g.astype(F32))
        cols.append(jnp.concatenate(gs, axis=0))
    gate = jnp.concatenate(cols, axis=1)
    w = (act * gate).astype(BF16)
    acc_ref[...] += jnp.dot(vt_ref[...], w, preferred_element_type=F32)

    @pl.when(e == pl.num_programs(1) - 1)
    def _():
        o_ref[...] = xres_ref[...] + g2_ref[0] * acc_ref[...].T


def peer(x3, norm_g, sc2, sh2, g2, wq, sub_keys, u_tab, v_tab):
    B, T, D = x3.shape
    N = B * T
    x = x3.reshape(N, D)
    n_hp = P_HEADS * 2
    n_exp = u_tab.shape[0]
    keys_b = sub_keys.reshape(n_hp, P_NKEYS, P_DHALF).astype(BF16)
    route_bidx = _mod_index(sc2.shape[0], PEER_ROUTE_TM, T)
    route_mod_spec = pl.BlockSpec((1, 1, D), lambda i: (route_bidx(i), 0, 0))
    mix_bidx = _mod_index(g2.shape[0], PEER_TM, T)
    route_tab = pl.BlockSpec((P_HEADS, P_NKEYS, PEER_ROUTE_TM), lambda i: (0, 0, i))
    e1, bmax, rank2, e2, xb = pl.pallas_call(
        _peer_route_kernel,
        grid=(N // PEER_ROUTE_TM,),
        in_specs=[pl.BlockSpec((PEER_ROUTE_TM, D), lambda i: (i, 0)),
                  pl.BlockSpec((1, D), lambda i: (0, 0)),
                  route_mod_spec, route_mod_spec,
                  pl.BlockSpec((D, P_HEADS * P_DKEY), lambda i: (0, 0)),
                  pl.BlockSpec((n_hp, P_NKEYS, P_DHALF), lambda i: (0, 0, 0))],
        out_specs=[route_tab, route_tab, route_tab, route_tab,
                   pl.BlockSpec((PEER_ROUTE_TM, D), lambda i: (i, 0))],
        out_shape=[jax.ShapeDtypeStruct((P_HEADS, P_NKEYS, N), F32),
                   jax.ShapeDtypeStruct((P_HEADS, P_NKEYS, N), F32),
                   jax.ShapeDtypeStruct((P_HEADS, P_NKEYS, N), BF16),
                   jax.ShapeDtypeStruct((P_HEADS, P_NKEYS, N), BF16),
                   jax.ShapeDtypeStruct((N, D), BF16)],
        compiler_params=pltpu.CompilerParams(
            dimension_semantics=("arbitrary",), vmem_limit_bytes=V7X_VMEM_LIMIT_BYTES),
        name="peer_route",
    )(x, norm_g.reshape(1, D), sc2, sh2, wq.astype(BF16), keys_b)

    mix_tab = pl.BlockSpec((P_HEADS, P_NKEYS, PEER_TM), lambda j, e: (0, 0, j))
    out = pl.pallas_call(
        _peer_mix_kernel,
        grid=(N // PEER_TM, n_exp // PEER_TE),
        in_specs=[pl.BlockSpec((PEER_TM, D), lambda j, e: (j, 0)),
                  mix_tab, mix_tab, mix_tab, mix_tab,
                  pl.BlockSpec((PEER_TE, D), lambda j, e: (e, 0)),
                  pl.BlockSpec((D, PEER_TE), lambda j, e: (0, e)),
                  pl.BlockSpec((PEER_TM, D), lambda j, e: (j, 0)),
                  pl.BlockSpec((1, 1, D), lambda j, e: (mix_bidx(j), 0, 0))],
        out_specs=pl.BlockSpec((PEER_TM, D), lambda j, e: (j, 0)),
        out_shape=jax.ShapeDtypeStruct((N, D), F32),
        scratch_shapes=[pltpu.VMEM((D, PEER_TM), F32)],
        compiler_params=pltpu.CompilerParams(
            dimension_semantics=("arbitrary", "arbitrary"), vmem_limit_bytes=V7X_VMEM_LIMIT_BYTES),
        name="peer_mix",
    )(xb, e1, bmax, rank2, e2, u_tab.astype(BF16), v_tab.astype(BF16).T, x, g2)
    return out.reshape(B, T, D)


PROJ_COL_TILE = 512


def _proj_layout():
    width = dict(zip(IN_NAMES, IN_SPLITS))
    start = dict(zip(IN_NAMES, np.cumsum((0,) + IN_SPLITS[:-1]).tolist()))
    perm, offs, pos = [], {}, 0
    for name in PROJ_ORDER:
        offs[name] = pos
        perm.extend(range(start[name], start[name] + width[name]))
        pos += width[name]
    return np.asarray(perm, np.int32), offs, pos + (-pos) % PROJ_COL_TILE


def trunk_layer(x, cvec, lp, ctx):
    B, T, _ = x.shape
    mod = jax.nn.silu(cvec) @ lp['w_mod'] + lp['b_mod']
    sh1, sc1, g1, sh2, sc2, g2 = [m[:, None, :] for m in jnp.split(mod, N_MOD, axis=-1)]
    perm, offs, n_proj = _proj_layout()
    w_in = jnp.pad(lp['w_in'][:, perm].astype(BF16), ((0, 0), (0, n_proj - len(perm))))
    proj = norm_proj(x.reshape(B * T, D_MODEL), lp['norm1_g'].reshape(1, D_MODEL), sc1, sh1, w_in, T)
    proj3 = proj.reshape(B, T, n_proj)
    (ar, ak, av, aw, aa, ag, bq, bk, bv, b_alpha, b_beta, bz, cq, ck, cv, gl) = [
        proj3[:, :, offs[n]:offs[n] + IN_SPLITS[IN_NAMES.index(n)]] for n in IN_NAMES]
    if ctx is None:
        s_rwkv0 = jnp.zeros((B, 2, A_HEADS, A_HD, A_HD), F32)
        s_gdn0 = jnp.zeros((B, 2, B_HEADS, B_HD, B_HD), F32)
        ctx_kv = None
    else:
        s_rwkv0 = ctx['rwkv'].astype(F32)
        s_gdn0 = ctx['gdn'].astype(F32)
        ctx_kv = (ctx['k'], ctx['v'])
    cols = {n: offs[n] // IN_SPLITS[IN_NAMES.index(n)] for n in ('ar', 'ak', 'av', 'aw', 'aa', 'ag')}
    ya, s_rwkv = rwkv7_branch(proj3, cols, lp, s_rwkv0)
    yb, s_gdn = gdn_branch(proj3, offs['bq'] // B_HD, offs['bz'] // B_W, b_alpha, b_beta, lp, s_gdn0)
    yc, k_c, v_c = attention_branch(cq, ck, cv, lp, ctx_kv)
    w_br = jnp.stack([lp['w_br_a'], lp['w_br_b'], lp['w_br_c']]).astype(BF16)
    x2 = merge_out(ya.reshape(B * T, A_W), yb.reshape(B * T, B_W), yc.reshape(B * T, C_W), proj,
                   offs['gl'] // D_MODEL, x.reshape(B * T, D_MODEL), g1, w_br, lp['w_out'].astype(BF16), T)
    x = peer(x2.reshape(B, T, D_MODEL), lp['norm2_g'], sc2, sh2, g2,
             lp['pk_wq'], lp['pk_keys'], lp['pk_u'], lp['pk_v'])
    return x, k_c, v_c, s_rwkv, s_gdn


_PARAM_NAMES = ('w_mod', 'b_mod', 'norm1_g', 'w_in',
                'rk_w_up', 'rk_w0', 'rk_a_up', 'rk_a0', 'rk_g_up', 'rk_kk', 'rk_ka', 'rk_rk', 'rk_ln_g', 'rk_ln_b',
                'gd_conv', 'gd_a_log', 'gd_dt_bias', 'gd_norm_g',
                'at_q_g', 'at_k_g',
                'w_br_a', 'w_br_b', 'w_br_c', 'w_out',
                'norm2_g', 'pk_wq', 'pk_keys', 'pk_u', 'pk_v')


def kernel(x_prompt, x_sample, cache_attn_k, cache_attn_v, state_rwkv, state_gdn, c, c_ctx, w_mod, b_mod, norm1_g, w_in, rk_w_up, rk_w0, rk_a_up, rk_a0, rk_g_up, rk_kk, rk_ka, rk_rk, rk_ln_g, rk_ln_b, gd_conv, gd_a_log, gd_dt_bias, gd_norm_g, at_q_g, at_k_g, w_br_a, w_br_b, w_br_c, w_out, norm2_g, pk_wq, pk_keys, pk_u, pk_v, final_g):
    stacked = dict(zip(_PARAM_NAMES, (
        w_mod, b_mod, norm1_g, w_in,
        rk_w_up, rk_w0, rk_a_up, rk_a0, rk_g_up, rk_kk, rk_ka, rk_rk, rk_ln_g, rk_ln_b,
        gd_conv, gd_a_log, gd_dt_bias, gd_norm_g,
        at_q_g, at_k_g,
        w_br_a, w_br_b, w_br_c, w_out,
        norm2_g, pk_wq, pk_keys, pk_u, pk_v)))
    xp = x_prompt
    xs = x_sample
    k_list, v_list, sr_list, sg_list = [], [], [], []
    for l in range(DEPTH):
        lp = {name: arr[l] for name, arr in stacked.items()}
        xp, k_c, v_c, s_r, s_g = trunk_layer(xp, c_ctx[None, :], lp, None)
        k_list.append(k_c)
        v_list.append(v_c)
        sr_list.append(s_r)
        sg_list.append(s_g)
        ctx = {'k': cache_attn_k[:, l], 'v': cache_attn_v[:, l], 'rwkv': state_rwkv[:, l], 'gdn': state_gdn[:, l]}
        xs = trunk_layer(xs, c, lp, ctx)[0]
    y_prompt = rms_norm(xp, final_g)
    y_sample = rms_norm(xs, final_g)
    return (y_prompt, y_sample, jnp.stack(k_list, axis=1), jnp.stack(v_list, axis=1),
            jnp.stack(sr_list, axis=1).astype(x_prompt.dtype), jnp.stack(sg_list, axis=1).astype(x_prompt.dtype))
```
